```python
import math
import jax, jax.numpy as jnp
from jax import lax
import numpy as np

D_MODEL = 1024
BATCH = 4
SEQ = 4096
DEPTH = 4
DEC_BATCH = 128
DEC_SEQ = 1
PAST_LEN = 2048
PAGE_SIZE = 128

N_META = 16
LAYER_KINDS = ('ssd', 'gdn', 'sba')
N_MIXERS = 3
CONV_W = 4
CHUNK = 64
Q_BLOCK = 128
EPS = 1e-5

SSD_DI = 2 * D_MODEL
SSD_HD = 64
SSD_H = SSD_DI // SSD_HD
SSD_G = 8
SSD_N = 128
SSD_CONV_DIM = SSD_DI + 2 * SSD_G * SSD_N
SSD_PROJ = SSD_DI + SSD_CONV_DIM + SSD_H

GDN_H = 8
GDN_DK = 128
GDN_DV = 256
GDN_QK = GDN_H * GDN_DK
GDN_VW = GDN_H * GDN_DV
GDN_CONV_DIM = 2 * GDN_QK + GDN_VW
GDN_PROJ = GDN_CONV_DIM + GDN_VW + 2 * GDN_H

SBA_H = 16
SBA_HD = D_MODEL // SBA_H
SBA_W = SBA_H * SBA_HD
SBA_PROJ = 4 * SBA_W
SB_BIAS_LO = -8.0
SB_BIAS_HI = -6.0

DN_ALPHA = (2 * DEPTH) ** 0.25
DN_BETA = (8 * DEPTH) ** -0.25
F32 = jnp.float32

kernel_name = 'hybrid_ssd_gdn_stickbreak_decoder_step'


def layer_norm(x, g, b):
    xf = x.astype(F32)
    mu = jnp.mean(xf, -1, keepdims=True)
    var = jnp.mean(jnp.square(xf - mu), -1, keepdims=True)
    return ((xf - mu) * lax.rsqrt(var + EPS) * g.astype(F32) + b.astype(F32)).astype(x.dtype)


def rms_norm(x, w):
    xf = x.astype(F32)
    return xf * lax.rsqrt(jnp.mean(xf * xf, -1, keepdims=True) + EPS) * w.astype(F32)


def l2_normalize(x):
    return x * lax.rsqrt(jnp.sum(x * x, -1, keepdims=True) + 1e-6)


def causal_conv(u, buf, w):
    full = jnp.concatenate([buf.astype(u.dtype), u], axis=1)
    out = lax.conv_general_dilated(full, w[:, None, :].astype(u.dtype), window_strides=(1,), padding='VALID',
                                   dimension_numbers=('NWC', 'WIO', 'NWC'), feature_group_count=u.shape[-1])
    return out, full[:, full.shape[1] - (CONV_W - 1):]


def run_segments(core, seq_args, const_args, h0, meta_len, chunk):
    if meta_len == 0:
        return core(*seq_args, *const_args, h0, chunk)
    y0, h = core(*[a[:, :meta_len] for a in seq_args], *const_args, h0, meta_len)
    y1, h = core(*[a[:, meta_len:] for a in seq_args], *const_args, h, chunk)
    return jnp.concatenate([y0, y1], axis=1), h


def ssd_chunked(x, dt, bm, cm, a_neg, h0, chunk):
    nb, L, H, P = x.shape
    G, N = bm.shape[2], bm.shape[3]
    R = H // G
    nc = L // chunk
    xdt = (x * dt[..., None]).reshape(nb, nc, chunk, G, R, P)
    a = (dt * a_neg).reshape(nb, nc, chunk, G, R).transpose(0, 1, 3, 4, 2)
    acum = jnp.cumsum(a, axis=-1)
    bc = bm.reshape(nb, nc, chunk, G, N)
    cc = cm.reshape(nb, nc, chunk, G, N)
    tri = jnp.tril(jnp.ones((chunk, chunk), bool))
    seg = jnp.exp(jnp.where(tri, acum[..., :, None] - acum[..., None, :], -jnp.inf))
    cb = jnp.einsum('bclgn,bcsgn->bcgls', cc, bc)
    y_diag = jnp.einsum('bcgls,bcgrls,bcsgrp->bclgrp', cb, seg, xdt)
    decay_to_end = jnp.exp(acum[..., -1:] - acum)
    chunk_states = jnp.einsum('bclgn,bcgrl,bclgrp->bcgrpn', bc, decay_to_end, xdt)
    chunk_decay = jnp.exp(acum[..., -1])

    def step(h, inp):
        s, d = inp
        return h * d[..., None, None] + s, h

    h_final, h_prev = lax.scan(step, h0.reshape(nb, G, R, P, N),
                               (jnp.moveaxis(chunk_states, 1, 0), jnp.moveaxis(chunk_decay, 1, 0)))
    h_prev = jnp.moveaxis(h_prev, 0, 1)
    y_off = jnp.einsum('bclgn,bcgrpn,bcgrl->bclgrp', cc, h_prev, jnp.exp(acum))
    return (y_diag + y_off).reshape(nb, L, H, P), h_final.reshape(nb, H, P, N)


def ssd_mixer(u, conv_buf, h0, w, meta_len, chunk):
    in_w, conv_w, conv_b, dt_bias, a_log, d_skip, norm_w, out_w = w
    nb, L, _ = u.shape
    proj = u @ in_w
    z = proj[..., :SSD_DI]
    xbc = proj[..., SSD_DI:SSD_DI + SSD_CONV_DIM]
    dt_raw = proj[..., SSD_DI + SSD_CONV_DIM:]
    xbc, new_buf = causal_conv(xbc, conv_buf, conv_w)
    xbc = jax.nn.silu((xbc + conv_b).astype(F32))
    xs = xbc[..., :SSD_DI].reshape(nb, L, SSD_H, SSD_HD)
    bm = xbc[..., SSD_DI:SSD_DI + SSD_G * SSD_N].reshape(nb, L, SSD_G, SSD_N)
    cm = xbc[..., SSD_DI + SSD_G * SSD_N:].reshape(nb, L, SSD_G, SSD_N)
    dt = jax.nn.softplus(dt_raw.astype(F32) + dt_bias.astype(F32))
    a_neg = -jnp.exp(a_log.astype(F32))
    y, h = run_segments(ssd_chunked, (xs, dt, bm, cm), (a_neg,), h0.astype(F32), meta_len, chunk)
    y = y + d_skip.astype(F32)[:, None] * xs
    y = rms_norm(y.reshape(nb, L, SSD_DI) * jax.nn.silu(z.astype(F32)), norm_w)
    return y.astype(u.dtype) @ out_w, h.astype(u.dtype), new_buf


def gdn_chunked(q, k, v, g, beta, s0, chunk):
    nb, L, H, DK = q.shape
    DV = v.shape[-1]
    nc = L // chunk

    def chunks(t):
        return jnp.moveaxis(t.reshape((nb, nc, chunk) + t.shape[2:]), 2, 3)

    qc, kc, vc = chunks(q), chunks(k), chunks(v)
    gc = jnp.cumsum(chunks(g), axis=-1)
    bc = chunks(beta)
    tri = jnp.tril(jnp.ones((chunk, chunk), bool))
    strict = jnp.tril(jnp.ones((chunk, chunk), bool), -1)
    decay = jnp.exp(jnp.where(tri, gc[..., :, None] - gc[..., None, :], -jnp.inf))
    kb = kc * bc[..., None]
    lower = jnp.where(strict, jnp.einsum('bchld,bchsd->bchls', kb, kc) * decay, 0.0)
    rhs = jnp.concatenate([vc * bc[..., None], kb * jnp.exp(gc)[..., None]], axis=-1)
    sol = lax.linalg.triangular_solve(jnp.eye(chunk, dtype=F32) + lower, rhs, left_side=True, lower=True)
    u_c, w_c = sol[..., :DV], sol[..., DV:]
    qk = jnp.einsum('bchld,bchsd->bchls', qc, kc) * decay
    g_last = gc[..., -1]

    def step(s, inp):
        qi, ki, ui, wi, qki, gi, gl = inp
        v_new = ui - jnp.einsum('bhld,bhdv->bhlv', wi, s)
        o = (jnp.einsum('bhld,bhdv->bhlv', qi * jnp.exp(gi)[..., None], s)
             + jnp.einsum('bhls,bhsv->bhlv', qki, v_new))
        s = (s * jnp.exp(gl)[..., None, None]
             + jnp.einsum('bhld,bhlv->bhdv', ki * jnp.exp(gl[..., None] - gi)[..., None], v_new))
        return s, o

    xs = tuple(jnp.moveaxis(t, 1, 0) for t in (qc, kc, u_c, w_c, qk, gc, g_last))
    s_final, o = lax.scan(step, s0, xs)
    o = jnp.moveaxis(jnp.moveaxis(o, 0, 1), 3, 2).reshape(nb, L, H, DV)
    return o, s_final


def gdn_mixer(u, conv_buf, s0, w, meta_len, chunk):
    in_w, conv_w, dt_bias, a_log, norm_w, out_w = w
    nb, L, _ = u.shape
    proj = u @ in_w
    qkv = proj[..., :GDN_CONV_DIM]
    z = proj[..., GDN_CONV_DIM:GDN_CONV_DIM + GDN_VW]
    a_raw = proj[..., GDN_CONV_DIM + GDN_VW:GDN_CONV_DIM + GDN_VW + GDN_H].astype(F32)
    b_raw = proj[..., GDN_CONV_DIM + GDN_VW + GDN_H:].astype(F32)
    qkv, new_buf = causal_conv(qkv, conv_buf, conv_w)
    qkv = jax.nn.silu(qkv.astype(F32))
    q = l2_normalize(qkv[..., :GDN_QK].reshape(nb, L, GDN_H, GDN_DK)) * GDN_DK ** -0.5
    k = l2_normalize(qkv[..., GDN_QK:2 * GDN_QK].reshape(nb, L, GDN_H, GDN_DK))
    v = qkv[..., 2 * GDN_QK:].reshape(nb, L, GDN_H, GDN_DV)
    g = -jnp.exp(a_log.astype(F32)) * jax.nn.softplus(a_raw + dt_bias.astype(F32))
    beta = jax.nn.sigmoid(b_raw)
    o, s = run_segments(gdn_chunked, (q, k, v, g, beta), (), s0.astype(F32), meta_len, chunk)
    o = rms_norm(o, norm_w) * jax.nn.silu(z.astype(F32).reshape(nb, L, GDN_H, GDN_DV))
    return o.reshape(nb, L, GDN_VW).astype(u.dtype) @ out_w, s.astype(u.dtype), new_buf


def sb_attend(q, qpos, k, v, kpos, sb_bias):
    z = (jnp.einsum('bqhd,bkhd->bhqk', q, k, preferred_element_type=F32) * SBA_HD ** -0.5
         + sb_bias.astype(F32)[None, :, None, None])
    vis = kpos[None, :] < qpos[:, None]
    log_beta = jax.nn.log_sigmoid(z)
    log_keep = jnp.where(vis, jax.nn.log_sigmoid(-z), 0.0)
    after = lax.cumsum(log_keep, axis=3, reverse=True) - log_keep
    a = jnp.where(vis, jnp.exp(log_beta + after), 0.0)
    return jnp.einsum('bhqk,bkhd->bqhd', a, v.astype(F32))


def sba_split(u, in_w):
    nb, L, _ = u.shape
    proj = u @ in_w
    q = proj[..., :SBA_W].reshape(nb, L, SBA_H, SBA_HD)
    k = proj[..., SBA_W:2 * SBA_W].reshape(nb, L, SBA_H, SBA_HD)
    v = proj[..., 2 * SBA_W:3 * SBA_W].reshape(nb, L, SBA_H, SBA_HD)
    z = proj[..., 3 * SBA_W:]
    return q, k, v, z


def sba_prompt(u, w, meta_len):
    in_w, sb_bias, out_w = w
    nb, L, _ = u.shape
    q, k, v, z = sba_split(u, in_w)
    kpos = jnp.arange(L)
    o_meta = sb_attend(q[:, :meta_len], kpos[:meta_len], k[:, :meta_len], v[:, :meta_len], kpos[:meta_len], sb_bias)
    n_blocks = (L - meta_len) // Q_BLOCK
    qb = jnp.moveaxis(q[:, meta_len:].reshape(nb, n_blocks, Q_BLOCK, SBA_H, SBA_HD), 1, 0)
    qpos = (meta_len + jnp.arange(L - meta_len)).reshape(n_blocks, Q_BLOCK)
    o_blocks = lax.map(lambda blk: sb_attend(blk[0], blk[1], k, v, kpos, sb_bias), (qb, qpos))
    o_main = jnp.moveaxis(o_blocks, 0, 1).reshape(nb, L - meta_len, SBA_H, SBA_HD)
    o = jnp.concatenate([o_meta, o_main], axis=1).reshape(nb, L, SBA_W)
    y = o * jax.nn.silu(z.astype(F32))
    return y.astype(u.dtype) @ out_w, k, v


def sba_sample(u, cache_k, cache_v, page_table, w):
    in_w, sb_bias, out_w = w
    nb, S, _ = u.shape
    q, k, v, z = sba_split(u, in_w)
    past = page_table.shape[1] * cache_k.shape[1]
    k_past = cache_k[page_table].reshape(nb, past, SBA_H, SBA_HD).astype(k.dtype)
    v_past = cache_v[page_table].reshape(nb, past, SBA_H, SBA_HD).astype(v.dtype)
    keys = jnp.concatenate([k_past, k], axis=1)
    vals = jnp.concatenate([v_past, v], axis=1)
    o = sb_attend(q, past + jnp.arange(S), keys, vals, jnp.arange(past + S), sb_bias).reshape(nb, S, SBA_W)
    y = o * jax.nn.silu(z.astype(F32))
    return y.astype(u.dtype) @ out_w, k, v


def _normal(key, shape, scale=1.0):
    return jax.random.normal(key, shape, F32) * scale


def _dense(key, fan_in, fan_out, scale=1.0):
    return jax.random.normal(key, (fan_in, fan_out), F32) * (scale * fan_in ** -0.5)


def _gain(key, n):
    return 1.0 + 0.01 * jax.random.normal(key, (n,), F32)


def _small(key, n):
    return 0.01 * jax.random.normal(key, (n,), F32)


def _dt_bias(key, n):
    dt = jnp.exp(jax.random.uniform(key, (n,), F32, math.log(1e-3), math.log(1e-1)))
    return dt + jnp.log(-jnp.expm1(-dt))


def _a_log(key, n):
    return jnp.log(jax.random.uniform(key, (n,), F32, 1.0, 16.0))


def _ssd_weights(key, p):
    k = jax.random.split(key, 10)
    return {p + 'in_w': _dense(k[0], D_MODEL, SSD_PROJ),
            p + 'conv_w': _normal(k[1], (CONV_W, SSD_CONV_DIM), CONV_W ** -0.5),
            p + 'conv_b': _small(k[2], SSD_CONV_DIM),
            p + 'dt_bias': _dt_bias(k[3], SSD_H),
            p + 'a_log': _a_log(k[4], SSD_H),
            p + 'd_skip': _gain(k[5], SSD_H),
            p + 'norm_w': _gain(k[6], SSD_DI),
            p + 'out_w': _dense(k[7], SSD_DI, D_MODEL, DN_BETA),
            p + 'ln_g': _gain(k[8], D_MODEL),
            p + 'ln_b': _small(k[9], D_MODEL)}


def _gdn_weights(key, p):
    k = jax.random.split(key, 8)
    return {p + 'in_w': _dense(k[0], D_MODEL, GDN_PROJ),
            p + 'conv_w': _normal(k[1], (CONV_W, GDN_CONV_DIM), CONV_W ** -0.5),
            p + 'dt_bias': _dt_bias(k[2], GDN_H),
            p + 'a_log': _a_log(k[3], GDN_H),
            p + 'norm_w': _gain(k[4], GDN_DV),
            p + 'out_w': _dense(k[5], GDN_VW, D_MODEL, DN_BETA),
            p + 'ln_g': _gain(k[6], D_MODEL),
            p + 'ln_b': _small(k[7], D_MODEL)}


def _sba_weights(key, p):
    k = jax.random.split(key, 5)
    return {p + 'in_w': _dense(k[0], D_MODEL, SBA_PROJ),
            p + 'sb_bias': jax.random.uniform(k[1], (SBA_H,), F32, SB_BIAS_LO, SB_BIAS_HI),
            p + 'out_w': _dense(k[2], SBA_W, D_MODEL, DN_BETA),
            p + 'ln_g': _gain(k[3], D_MODEL),
            p + 'ln_b': _small(k[4], D_MODEL)}


def setup_inputs(seed: int = 0) -> dict:
    key = jax.random.key(seed)
    ks = jax.random.split(key, 16)
    n_pages = PAST_LEN // PAGE_SIZE
    used = DEC_BATCH * n_pages
    n_pool = used + max(1, used // 4)
    inp = {}
    inp['x_prompt'] = _normal(ks[0], (BATCH, SEQ, D_MODEL))
    inp['x_sample'] = _normal(ks[1], (DEC_BATCH, DEC_SEQ, D_MODEL))
    inp['state_l0_ssm'] = _normal(ks[2], (DEC_BATCH, SSD_H, SSD_HD, SSD_N), 0.5)
    inp['state_l0_conv'] = _normal(ks[3], (DEC_BATCH, CONV_W - 1, SSD_CONV_DIM))
    inp['state_l1_delta'] = _normal(ks[4], (DEC_BATCH, GDN_H, GDN_DK, GDN_DV), 0.5)
    inp['state_l1_conv'] = _normal(ks[5], (DEC_BATCH, CONV_W - 1, GDN_CONV_DIM))
    inp['cache_l2_k'] = _normal(ks[6], (n_pool, PAGE_SIZE, SBA_H, SBA_HD))
    inp['cache_l2_v'] = _normal(ks[7], (n_pool, PAGE_SIZE, SBA_H, SBA_HD))
    inp['state_l3_ssm'] = _normal(ks[8], (DEC_BATCH, SSD_H, SSD_HD, SSD_N), 0.5)
    inp['state_l3_conv'] = _normal(ks[9], (DEC_BATCH, CONV_W - 1, SSD_CONV_DIM))
    inp['page_table'] = jax.random.permutation(ks[10], n_pool)[:used].reshape(DEC_BATCH, n_pages).astype(jnp.int32)
    inp['meta_tokens'] = _normal(ks[11], (N_META, D_MODEL))
    inp.update(_ssd_weights(ks[12], 'l0_'))
    inp.update(_gdn_weights(ks[13], 'l1_'))
    inp.update(_sba_weights(ks[14], 'l2_'))
    inp.update(_ssd_weights(ks[15], 'l3_'))
    return inp


def reference(x_prompt, x_sample, state_l0_ssm, state_l0_conv, state_l1_delta, state_l1_conv,
              cache_l2_k, cache_l2_v, state_l3_ssm, state_l3_conv, page_table, meta_tokens,
              l0_in_w, l0_conv_w, l0_conv_b, l0_dt_bias, l0_a_log, l0_d_skip, l0_norm_w, l0_out_w, l0_ln_g, l0_ln_b,
              l1_in_w, l1_conv_w, l1_dt_bias, l1_a_log, l1_norm_w, l1_out_w, l1_ln_g, l1_ln_b,
              l2_in_w, l2_sb_bias, l2_out_w, l2_ln_g, l2_ln_b,
              l3_in_w, l3_conv_w, l3_conv_b, l3_dt_bias, l3_a_log, l3_d_skip, l3_norm_w, l3_out_w, l3_ln_g, l3_ln_b):
    weights = [
        (l0_in_w, l0_conv_w, l0_conv_b, l0_dt_bias, l0_a_log, l0_d_skip, l0_norm_w, l0_out_w),
        (l1_in_w, l1_conv_w, l1_dt_bias, l1_a_log, l1_norm_w, l1_out_w),
        (l2_in_w, l2_sb_bias, l2_out_w),
        (l3_in_w, l3_conv_w, l3_conv_b, l3_dt_bias, l3_a_log, l3_d_skip, l3_norm_w, l3_out_w),
    ]
    norms = [(l0_ln_g, l0_ln_b), (l1_ln_g, l1_ln_b), (l2_ln_g, l2_ln_b), (l3_ln_g, l3_ln_b)]
    states = [(state_l0_ssm, state_l0_conv), (state_l1_delta, state_l1_conv),
              (cache_l2_k, cache_l2_v), (state_l3_ssm, state_l3_conv)]

    nb = x_prompt.shape[0]
    meta = jnp.broadcast_to(meta_tokens[None].astype(x_prompt.dtype), (nb, N_META, D_MODEL))
    xp = jnp.concatenate([meta, x_prompt], axis=1)
    xs = x_sample
    p_new = []
    s_new = []
    for i in range(DEPTH):
        kind = LAYER_KINDS[i % N_MIXERS]
        w = weights[i]
        st_a, st_b = states[i]
        if kind == 'ssd':
            fp, pa, pb = ssd_mixer(xp, jnp.zeros((nb, CONV_W - 1, SSD_CONV_DIM), xp.dtype),
                                   jnp.zeros((nb, SSD_H, SSD_HD, SSD_N), xp.dtype), w, N_META, CHUNK)
            fs, sa, sb = ssd_mixer(xs, st_b, st_a, w, 0, xs.shape[1])
        elif kind == 'gdn':
            fp, pa, pb = gdn_mixer(xp, jnp.zeros((nb, CONV_W - 1, GDN_CONV_DIM), xp.dtype),
                                   jnp.zeros((nb, GDN_H, GDN_DK, GDN_DV), xp.dtype), w, N_META, CHUNK)
            fs, sa, sb = gdn_mixer(xs, st_b, st_a, w, 0, xs.shape[1])
        else:
            fp, pa, pb = sba_prompt(xp, w, N_META)
            fs, sa, sb = sba_sample(xs, st_a, st_b, page_table, w)
        p_new += [pa, pb]
        s_new += [sa, sb]
        ln_g, ln_b = norms[i]
        xp = layer_norm(DN_ALPHA * xp + fp, ln_g, ln_b)
        xs = layer_norm(DN_ALPHA * xs + fs, ln_g, ln_b)

    y_prompt = xp[:, N_META:]
    y_sample = xs
    p_l0_ssm, p_l0_conv, p_l1_delta, p_l1_conv, p_l2_k, p_l2_v, p_l3_ssm, p_l3_conv = p_new
    s_l0_ssm, s_l0_conv, s_l1_delta, s_l1_conv, s_l2_k, s_l2_v, s_l3_ssm, s_l3_conv = s_new
    return (y_prompt, y_sample,
            p_l0_ssm, p_l0_conv, p_l1_delta, p_l1_conv, p_l2_k, p_l2_v, p_l3_ssm, p_l3_conv,
            s_l0_ssm, s_l0_conv, s_l1_delta, s_l1_conv, s_l2_k, s_l2_v, s_l3_ssm, s_l3_conv)
```

```python
import functools
import math

import jax
import jax.numpy as jnp
from jax import lax
from jax.experimental import pallas as pl
from jax.experimental.pallas import tpu as pltpu

F32 = jnp.float32
BF16 = jnp.bfloat16

N_META = 16
CONV_W = 4
EPS = 1e-5
D_MODEL = 1024
SSD_DI = 2048
SSD_HD = 64
SSD_H = 32
SSD_G = 8
SSD_N = 128
SSD_R = SSD_H // SSD_G
SSD_CONV = SSD_DI + 2 * SSD_G * SSD_N
GDN_H = 8
GDN_DK = 128
GDN_DV = 256
GDN_QK = GDN_H * GDN_DK
GDN_VW = GDN_H * GDN_DV
GDN_CONV = 2 * GDN_QK + GDN_VW
SBA_H = 16
SBA_HD = 64
SBA_W = SBA_H * SBA_HD
DEPTH = 4
DN_ALPHA = (2 * DEPTH) ** 0.25

LANES = 128
CHUNK = 128
VMEM_LIMIT = 56 * 1024 * 1024


def _cparams(sem):
    return pltpu.CompilerParams(dimension_semantics=sem, vmem_limit_bytes=VMEM_LIMIT)


def _dot(a, b):
    return jnp.dot(a, b, preferred_element_type=F32)


def _dot_nt(a, b):
    return lax.dot_general(a, b, (((1,), (1,)), ((), ())), preferred_element_type=F32)


def _split(x, terms):
    parts = []
    r = x
    for i in range(terms):
        p = r.astype(BF16)
        parts.append(p)
        if i + 1 < terms:
            r = r - p.astype(F32)
    return parts


def _dot_x01(x, e, terms=3):
    m = x.shape[0]
    parts = _split(x, terms)
    r = _dot(jnp.concatenate(parts, axis=0), e)
    out = r[0:m]
    for i in range(1, terms):
        out = out + r[i * m:(i + 1) * m]
    return out


def _dot_01x(e, x, terms=3):
    n = x.shape[1]
    parts = _split(x, terms)
    r = _dot(e, jnp.concatenate(parts, axis=1))
    out = r[:, 0:n]
    for i in range(1, terms):
        out = out + r[:, i * n:(i + 1) * n]
    return out


def _sigmoid(x):
    return 1.0 / (1.0 + jnp.exp(-x))


def _silu(x):
    return x * _sigmoid(x)


def _softplus(x):
    return jnp.maximum(x, 0.0) + jnp.log1p(jnp.exp(-jnp.abs(x)))


def _tri(n, strict):
    r = lax.broadcasted_iota(jnp.int32, (n, n), 0)
    c = lax.broadcasted_iota(jnp.int32, (n, n), 1)
    return (c < r) if strict else (c <= r)


def _mm_kernel(x_ref, w_ref, o_ref, xb_ref):
    @pl.when(pl.program_id(1) == 0)
    def _():
        xb_ref[...] = x_ref[...].astype(BF16)

    o_ref[...] = _dot(xb_ref[...], w_ref[...]).astype(o_ref.dtype)


def _pick(n, cands):
    for c in cands:
        if n % c == 0:
            return c
    return n


def _matmul(x, w, name):
    m, k = x.shape
    n = w.shape[1]
    tm = _pick(m, (1536, 1024, 768, 512, 384, 256, 128))
    tn = _pick(n, (1024, 512, 256, 128))
    return pl.pallas_call(
        _mm_kernel,
        out_shape=jax.ShapeDtypeStruct((m, n), F32),
        grid=(m // tm, n // tn),
        in_specs=[pl.BlockSpec((tm, k), lambda i, j: (i, 0)),
                  pl.BlockSpec((k, tn), lambda i, j: (0, j))],
        out_specs=pl.BlockSpec((tm, tn), lambda i, j: (i, j)),
        scratch_shapes=[pltpu.VMEM((tm, k), BF16)],
        compiler_params=_cparams(("parallel", "arbitrary")),
        name=name,
    )(x, w)


def _out_ln_kernel(y_ref, w_ref, x_ref, g_ref, b_ref, o_ref):
    f = _dot(y_ref[...].astype(BF16), w_ref[...])
    r = DN_ALPHA * x_ref[...] + f
    mu = jnp.mean(r, axis=-1, keepdims=True)
    d = r - mu
    var = jnp.mean(d * d, axis=-1, keepdims=True)
    o_ref[...] = d * lax.rsqrt(var + EPS) * g_ref[...] + b_ref[...]


def _out_ln(y, w, x, g, b, name):
    m, k = y.shape
    d = w.shape[1]
    tm = _pick(m, (512, 384, 256, 128))
    return pl.pallas_call(
        _out_ln_kernel,
        out_shape=jax.ShapeDtypeStruct((m, d), F32),
        grid=(m // tm,),
        in_specs=[pl.BlockSpec((tm, k), lambda i: (i, 0)),
                  pl.BlockSpec((k, d), lambda i: (0, 0)),
                  pl.BlockSpec((tm, d), lambda i: (i, 0)),
                  pl.BlockSpec((1, d), lambda i: (0, 0)),
                  pl.BlockSpec((1, d), lambda i: (0, 0))],
        out_specs=pl.BlockSpec((tm, d), lambda i: (i, 0)),
        compiler_params=_cparams(("parallel",)),
        name=name,
    )(y, w, x, g.reshape(1, d), b.reshape(1, d))


def _conv_chunk(ext_ref, parts, valid, c, t, convw_ref, tail_ref, is_last):
    ncol = ext_ref.shape[1]

    @pl.when(c == 0)
    def _():
        ext_ref[0:8, :] = jnp.zeros((8, ncol), F32)

    @pl.when(c > 0)
    def _():
        ext_ref[0:8, :] = ext_ref[t:t + 8, :]

    off = 0
    for p in parts:
        w = p.shape[1]
        ext_ref[8:8 + t, off:off + w] = jnp.where(valid, p[...], 0.0)
        off += w

    @pl.when(is_last)
    def _():
        tail_ref[0] = ext_ref[t + 5:t + 8, :]

    acc = convw_ref[0:1, :] * ext_ref[5:5 + t, :]
    for j in range(1, CONV_W):
        acc = acc + convw_ref[j:j + 1, :] * ext_ref[5 + j:5 + j + t, :]
    return acc


def _ssd_prompt_kernel(pad, z_ref, xs_ref, bc_ref, x_ref, wdt_ref, convw_ref, convb_ref,
                       dtb_ref, aneg_ref, dskip_ref, normw_ref, expand_ref,
                       yn_ref, hfin_ref, tail_ref, ext_ref, ht_ref, y_ref):
    c = pl.program_id(1)
    is_last = c == pl.num_programs(1) - 1
    t = z_ref.shape[0]
    gw = SSD_R * SSD_HD

    @pl.when(c == 0)
    def _():
        ht_ref[...] = jnp.zeros(ht_ref.shape, F32)

    row = lax.broadcasted_iota(jnp.int32, (t, 1), 0)
    valid = jnp.logical_or(c > 0, row >= pad)

    conv = _conv_chunk(ext_ref, (xs_ref, bc_ref), valid, c, t, convw_ref, tail_ref, is_last)
    xbc = _silu(conv + convb_ref[...])
    xs = xbc[:, :SSD_DI]
    bm = xbc[:, SSD_DI:SSD_DI + SSD_G * SSD_N]
    cm = xbc[:, SSD_DI + SSD_G * SSD_N:]

    dt_raw = _dot(x_ref[...].astype(BF16), wdt_ref[...])
    dt = jnp.where(valid, _softplus(dt_raw + dtb_ref[...]), 0.0)
    a = dt * aneg_ref[...]
    ltri = jnp.where(_tri(t, False), 1.0, 0.0).astype(BF16)
    acum = _dot_01x(ltri, a)
    acum_t = acum.T

    both = _dot_x01(jnp.concatenate([dt, acum], axis=0), expand_ref[...])
    dt_e = both[0:t]
    acum_e = both[t:2 * t]
    last_e = acum_e[t - 1:t, :]
    eac_e = jnp.exp(acum_e)
    dte_e = jnp.exp(last_e - acum_e)
    cd_e = jnp.exp(last_e)

    xdt = xs * dt_e
    xw = xdt * dte_e
    tri = _tri(t, False)
    lane_g = lax.broadcasted_iota(jnp.int32, (1, gw), 1) // SSD_HD

    for g in range(SSD_G):
        gs = slice(g * gw, (g + 1) * gw)
        ns = slice(g * SSD_N, (g + 1) * SSD_N)
        bg = bm[:, ns]
        cg = cm[:, ns].astype(BF16)
        cb = _dot_nt(cg, bg.astype(BF16))
        xdt_g = xdt[:, gs].astype(BF16)
        ht = ht_ref[g]
        y_g = _dot(cg, ht.astype(BF16)) * eac_e[:, gs]
        for r in range(SSD_R):
            h = g * SSD_R + r
            diff = acum[:, h:h + 1] - acum_t[h:h + 1, :]
            seg = jnp.exp(jnp.where(tri, diff, -jnp.inf))
            yd = _dot((cb * seg).astype(BF16), xdt_g)
            y_g = y_g + jnp.where(lane_g == r, yd, 0.0)
        y_g = y_g + dskip_ref[:, gs] * xs[:, gs]
        y_ref[:, gs] = y_g
        ht_ref[g] = ht * cd_e[:, gs] + _dot(bg.T.astype(BF16), xw[:, gs].astype(BF16))

    y = y_ref[...] * _silu(z_ref[...])
    ms = jnp.mean(y * y, axis=-1, keepdims=True)
    yn_ref[...] = (y * lax.rsqrt(ms + EPS) * normw_ref[...]).astype(yn_ref.dtype)

    @pl.when(is_last)
    def _():
        for g in range(SSD_G):
            hfin_ref[0, g * gw:(g + 1) * gw, :] = ht_ref[g].T


def _ssd_consts(w):
    in_w, conv_w, conv_b, dt_bias, a_log, d_skip, norm_w, out_w = w
    main = in_w[:, :SSD_DI + SSD_CONV].astype(BF16)
    wdt = jnp.pad(in_w[:, SSD_DI + SSD_CONV:], ((0, 0), (0, LANES - SSD_H))).astype(BF16)
    dtb = jnp.pad(dt_bias.astype(F32), (0, LANES - SSD_H)).reshape(1, LANES)
    aneg = jnp.pad(-jnp.exp(a_log.astype(F32)), (0, LANES - SSD_H)).reshape(1, LANES)
    dskip_e = jnp.repeat(d_skip.astype(F32), SSD_HD).reshape(1, SSD_DI)
    hh = lax.broadcasted_iota(jnp.int32, (LANES, SSD_DI), 0)
    cc = lax.broadcasted_iota(jnp.int32, (LANES, SSD_DI), 1) // SSD_HD
    expand = (hh == cc).astype(BF16)
    return dict(main=main, wdt=wdt, dtb=dtb, aneg=aneg, dskip_e=dskip_e, expand=expand,
                conv_w=conv_w.astype(F32), conv_b=conv_b.astype(F32).reshape(1, SSD_CONV),
                norm_w=norm_w.astype(F32).reshape(1, SSD_DI), out_w=out_w.astype(BF16))


def _ssd_prompt(xp, cs, pad, nb, lp, name):
    t = CHUNK
    nc = lp // t
    proj = _matmul(xp, cs["main"], name + "_in")
    const = lambda b, c: (0, 0)
    rows = lambda b, c: (b * nc + c, 0)
    yn, hfin, tail = pl.pallas_call(
        functools.partial(_ssd_prompt_kernel, pad),
        out_shape=(jax.ShapeDtypeStruct((nb * lp, SSD_DI), BF16),
                   jax.ShapeDtypeStruct((nb, SSD_DI, SSD_N), F32),
                   jax.ShapeDtypeStruct((nb, CONV_W - 1, SSD_CONV), F32)),
        grid=(nb, nc),
        in_specs=[pl.BlockSpec((t, SSD_DI), lambda b, c: (b * nc + c, 0)),
                  pl.BlockSpec((t, SSD_DI), lambda b, c: (b * nc + c, 1)),
                  pl.BlockSpec((t, SSD_DI), lambda b, c: (b * nc + c, 2)),
                  pl.BlockSpec((t, D_MODEL), rows),
                  pl.BlockSpec((D_MODEL, LANES), const),
                  pl.BlockSpec((CONV_W, SSD_CONV), const),
                  pl.BlockSpec((1, SSD_CONV), const),
                  pl.BlockSpec((1, LANES), const),
                  pl.BlockSpec((1, LANES), const),
                  pl.BlockSpec((1, SSD_DI), const),
                  pl.BlockSpec((1, SSD_DI), const),
                  pl.BlockSpec((LANES, SSD_DI), const)],
        out_specs=(pl.BlockSpec((t, SSD_DI), rows),
                   pl.BlockSpec((1, SSD_DI, SSD_N), lambda b, c: (b, 0, 0)),
                   pl.BlockSpec((1, CONV_W - 1, SSD_CONV), lambda b, c: (b, 0, 0))),
        scratch_shapes=[pltpu.VMEM((t + 8, SSD_CONV), F32),
                        pltpu.VMEM((SSD_G, SSD_N, SSD_R * SSD_HD), F32),
                        pltpu.VMEM((t, SSD_DI), F32)],
        compiler_params=_cparams(("parallel", "arbitrary")),
        name=name,
    )(proj, proj, proj, xp, cs["wdt"], cs["conv_w"], cs["conv_b"], cs["dtb"], cs["aneg"],
      cs["dskip_e"], cs["norm_w"], cs["expand"])
    return yn, hfin.reshape(nb, SSD_H, SSD_HD, SSD_N), tail


INV_BLOCK = 16


def _unit_lower_inverse(a, t):
    r = lax.broadcasted_iota(jnp.int32, (t, t), 0)
    c = lax.broadcasted_iota(jnp.int32, (t, t), 1)
    b = INV_BLOCK
    ad = jnp.where((r // b) == (c // b), a, 0.0)
    p = jnp.where(r == c, 1.0, 0.0) - ad
    ap = ad
    for _ in range(int(math.log2(b)) - 1):
        apb = ap.astype(BF16)
        ap = _dot(apb, apb)
        p = p + _dot(p.astype(BF16), ap.astype(BF16))
    while b < t:
        off = jnp.where(((r // (2 * b)) == (c // (2 * b))) & ((r // b) != (c // b)), a, 0.0)
        pb = p.astype(BF16)
        p = p - _dot(_dot(pb, off.astype(BF16)).astype(BF16), pb)
        b *= 2
    return p


def _gdn_prompt_kernel(pad, q_ref, k_ref, v_ref, z_ref, x_ref, wa_ref, wb_ref, convw_ref,
                       dtb_ref, aneg_ref, normw_ref,
                       on_ref, sfin_ref, tail_ref, ext_ref, s_ref):
    c = pl.program_id(1)
    is_last = c == pl.num_programs(1) - 1
    t = q_ref.shape[0]

    @pl.when(c == 0)
    def _():
        s_ref[...] = jnp.zeros(s_ref.shape, F32)

    row = lax.broadcasted_iota(jnp.int32, (t, 1), 0)
    valid = jnp.logical_or(c > 0, row >= pad)

    conv = _conv_chunk(ext_ref, (q_ref, k_ref, v_ref), valid, c, t, convw_ref, tail_ref, is_last)
    qkv = _silu(conv)

    xb = x_ref[...].astype(BF16)
    a_raw = _dot(xb, wa_ref[...])
    b_raw = _dot(xb, wb_ref[...])
    gate = jnp.where(valid, aneg_ref[...] * _softplus(a_raw + dtb_ref[...]), 0.0)
    beta = jnp.where(valid, _sigmoid(b_raw), 0.0)
    ltri = jnp.where(_tri(t, False), 1.0, 0.0).astype(BF16)
    gc = _dot_01x(ltri, gate)
    gc_t = gc.T
    tri = _tri(t, False)
    strict = _tri(t, True)

    for h in range(GDN_H):
        qs = slice(h * GDN_DK, (h + 1) * GDN_DK)
        ks = slice(GDN_QK + h * GDN_DK, GDN_QK + (h + 1) * GDN_DK)
        vs = slice(2 * GDN_QK + h * GDN_DV, 2 * GDN_QK + (h + 1) * GDN_DV)
        q = qkv[:, qs]
        k = qkv[:, ks]
        v = qkv[:, vs]
        q = q * lax.rsqrt(jnp.sum(q * q, axis=-1, keepdims=True) + 1e-6) * (GDN_DK ** -0.5)
        k = k * lax.rsqrt(jnp.sum(k * k, axis=-1, keepdims=True) + 1e-6)
        gcol = gc[:, h:h + 1]
        bcol = beta[:, h:h + 1]
        dec = jnp.exp(jnp.where(tri, gcol - gc_t[h:h + 1, :], -jnp.inf))
        kb = k * bcol
        kbf = k.astype(BF16)
        a = jnp.where(strict, _dot_nt(kb.astype(BF16), kbf) * dec, 0.0)
        tinv = _unit_lower_inverse(a, t).astype(BF16)
        eg = jnp.exp(gcol)
        u = _dot(tinv, (v * bcol).astype(BF16))
        w = _dot(tinv, (kb * eg).astype(BF16))
        qk = _dot_nt(q.astype(BF16), kbf) * dec
        s = s_ref[h]
        sb = s.astype(BF16)
        v_new = u - _dot(w.astype(BF16), sb)
        vnb = v_new.astype(BF16)
        o = _dot((q * eg).astype(BF16), sb) + _dot(qk.astype(BF16), vnb)
        gl = gc[t - 1:t, h:h + 1]
        kd = k * jnp.exp(gl - gcol)
        s_ref[h] = s * jnp.exp(gl) + _dot(kd.T.astype(BF16), vnb)
        ms = jnp.mean(o * o, axis=-1, keepdims=True)
        zh = z_ref[:, h * GDN_DV:(h + 1) * GDN_DV]
        on_ref[:, h * GDN_DV:(h + 1) * GDN_DV] = (
            o * lax.rsqrt(ms + EPS) * normw_ref[...] * _silu(zh)).astype(on_ref.dtype)

    @pl.when(is_last)
    def _():
        sfin_ref[0] = s_ref[...]


def _gdn_consts(w):
    in_w, conv_w, dt_bias, a_log, norm_w, out_w = w
    nmain = GDN_CONV + GDN_VW
    main = in_w[:, :nmain].astype(BF16)
    wa = jnp.pad(in_w[:, nmain:nmain + GDN_H], ((0, 0), (0, LANES - GDN_H))).astype(BF16)
    wb = jnp.pad(in_w[:, nmain + GDN_H:], ((0, 0), (0, LANES - GDN_H))).astype(BF16)
    dtb = jnp.pad(dt_bias.astype(F32), (0, LANES - GDN_H)).reshape(1, LANES)
    aneg = jnp.pad(-jnp.exp(a_log.astype(F32)), (0, LANES - GDN_H)).reshape(1, LANES)
    return dict(main=main, wa=wa, wb=wb, dtb=dtb, aneg=aneg, conv_w=conv_w.astype(F32),
                norm_w=norm_w.astype(F32).reshape(1, GDN_DV), out_w=out_w.astype(BF16))


def _gdn_prompt(xp, cs, pad, nb, lp, name):
    t = CHUNK
    nc = lp // t
    proj = _matmul(xp, cs["main"], name + "_in")
    const = lambda b, c: (0, 0)
    rows = lambda b, c: (b * nc + c, 0)
    on, sfin, tail = pl.pallas_call(
        functools.partial(_gdn_prompt_kernel, pad),
        out_shape=(jax.ShapeDtypeStruct((nb * lp, GDN_VW), BF16),
                   jax.ShapeDtypeStruct((nb, GDN_H, GDN_DK, GDN_DV), F32),
                   jax.ShapeDtypeStruct((nb, CONV_W - 1, GDN_CONV), F32)),
        grid=(nb, nc),
        in_specs=[pl.BlockSpec((t, GDN_QK), lambda b, c: (b * nc + c, 0)),
                  pl.BlockSpec((t, GDN_QK), lambda b, c: (b * nc + c, 1)),
                  pl.BlockSpec((t, GDN_VW), lambda b, c: (b * nc + c, 1)),
                  pl.BlockSpec((t, GDN_VW), lambda b, c: (b * nc + c, 2)),
                  pl.BlockSpec((t, D_MODEL), rows),
                  pl.BlockSpec((D_MODEL, LANES), const),
                  pl.BlockSpec((D_MODEL, LANES), const),
                  pl.BlockSpec((CONV_W, GDN_CONV), const),
                  pl.BlockSpec((1, LANES), const),
                  pl.BlockSpec((1, LANES), const),
                  pl.BlockSpec((1, GDN_DV), const)],
        out_specs=(pl.BlockSpec((t, GDN_VW), rows),
                   pl.BlockSpec((1, GDN_H, GDN_DK, GDN_DV), lambda b, c: (b, 0, 0, 0)),
                   pl.BlockSpec((1, CONV_W - 1, GDN_CONV), lambda b, c: (b, 0, 0))),
        scratch_shapes=[pltpu.VMEM((t + 8, GDN_CONV), F32),
                        pltpu.VMEM((GDN_H, GDN_DK, GDN_DV), F32)],
        compiler_params=_cparams(("parallel", "arbitrary")),
        name=name,
    )(proj, proj, proj, proj, xp, cs["wa"], cs["wb"], cs["conv_w"], cs["dtb"], cs["aneg"],
      cs["norm_w"])
    return on, sfin, tail


def _sba_prompt_kernel(pad, bias_ref, q_ref, k_ref, v_ref, z_ref, y_ref):
    hp = pl.program_id(1)
    qi = pl.program_id(2)
    t = q_ref.shape[0]
    scale = SBA_HD ** -0.5
    first = lax.broadcasted_iota(jnp.int32, (1, 2 * SBA_HD), 1) < SBA_HD
    q = q_ref[...]
    q_a = jnp.where(first, q, 0.0).astype(BF16)
    q_b = jnp.where(first, 0.0, q).astype(BF16)
    bias_a = bias_ref[2 * hp]
    bias_b = bias_ref[2 * hp + 1]
    later = jnp.where(_tri(t, True), 1.0, 0.0).astype(BF16)
    qpos = qi * t + lax.broadcasted_iota(jnp.int32, (t, 1), 0)
    klane = lax.broadcasted_iota(jnp.int32, (1, t), 1)

    def body(jj, carry):
        o_a, o_b, c_a, c_b = carry
        j = qi - jj
        start = pl.multiple_of(j * t, t)
        kb = k_ref[pl.ds(start, t), :].astype(BF16)
        vb = v_ref[pl.ds(start, t), :].astype(BF16)
        kpos = j * t + klane
        vis = jnp.logical_and(kpos < qpos, kpos >= pad)

        def head(qh, bias, o, car):
            z = _dot_nt(qh, kb) * scale + bias
            sp = _softplus(z)
            lk = jnp.where(vis, -sp, 0.0)
            aft = _dot_x01(lk, later, terms=2) + car
            a = jnp.where(vis, jnp.exp(z - sp + aft), 0.0)
            return o + _dot(a.astype(BF16), vb), car + jnp.sum(lk, axis=-1, keepdims=True)

        o_a, c_a = head(q_a, bias_a, o_a, c_a)
        o_b, c_b = head(q_b, bias_b, o_b, c_b)
        return o_a, o_b, c_a, c_b

    zo = jnp.zeros((t, 2 * SBA_HD), F32)
    zc = jnp.zeros((t, 1), F32)
    o_a, o_b, _, _ = lax.fori_loop(0, qi + 1, body, (zo, zo, zc, zc))
    o = jnp.where(first, o_a, o_b)
    y_ref[...] = (o * _silu(z_ref[...])).astype(y_ref.dtype)


def _sba_prompt(xp, in_w, sb_bias, pad, nb, lp, name):
    t = CHUNK
    nq = lp // t
    hw = 2 * SBA_HD
    npair = SBA_H // 2
    proj = _matmul(xp, in_w, name + "_in")
    y = pl.pallas_call(
        functools.partial(_sba_prompt_kernel, pad),
        out_shape=jax.ShapeDtypeStruct((nb * lp, SBA_W), BF16),
        grid=(nb, npair, nq),
        in_specs=[pl.BlockSpec(memory_space=pltpu.SMEM),
                  pl.BlockSpec((t, hw), lambda b, p, i: (b * nq + i, p)),
                  pl.BlockSpec((lp, hw), lambda b, p, i: (b, npair + p)),
                  pl.BlockSpec((lp, hw), lambda b, p, i: (b, 2 * npair + p)),
                  pl.BlockSpec((t, hw), lambda b, p, i: (b * nq + i, 3 * npair + p))],
        out_specs=pl.BlockSpec((t, hw), lambda b, p, i: (b * nq + i, p)),
        compiler_params=_cparams(("parallel", "parallel", "arbitrary")),
        name=name,
    )(sb_bias.astype(F32), proj, proj, proj, proj)
    return y, proj


DEC_ROWS = 32
DEC_SEQS = 8


def _dec_conv(parts, st_ref, convw_ref, newst_ref):
    ncol = convw_ref.shape[1]
    raw = jnp.concatenate([p[...] for p in parts], axis=1)
    conv = convw_ref[CONV_W - 1:CONV_W, :] * raw
    for j in range(CONV_W - 1):
        conv = conv + convw_ref[j:j + 1, :] * st_ref[:, j * ncol:(j + 1) * ncol]
    newst_ref[:, 0:(CONV_W - 2) * ncol] = st_ref[:, ncol:(CONV_W - 1) * ncol]
    newst_ref[:, (CONV_W - 2) * ncol:] = raw
    return conv


def _ssd_dec_pre_kernel(xs_ref, bc_ref, x_ref, st_ref, wdt_ref, convw_ref, convb_ref, dtb_ref,
                        aneg_ref, expand_ref, newst_ref, xsa_ref, xdt_ref, bca_ref, dec_ref):
    conv = _dec_conv((xs_ref, bc_ref), st_ref, convw_ref, newst_ref)
    xbc = _silu(conv + convb_ref[...])
    dt = _softplus(_dot(x_ref[...].astype(BF16), wdt_ref[...]) + dtb_ref[...])
    dec_ref[...] = jnp.exp(dt * aneg_ref[...])
    xsa = xbc[:, :SSD_DI]
    xsa_ref[...] = xsa
    xdt_ref[...] = xsa * _dot_x01(dt, expand_ref[...])
    bca_ref[...] = xbc[:, SSD_DI:]


def _ssd_dec_state_kernel(dec_ref, h_ref, xdt_ref, bm_ref, cm_ref, xsa_ref, z_ref, dskip_ref,
                          normw_ref, hout_ref, yn_ref):
    blk = pl.program_id(0)
    nseq = h_ref.shape[0]
    gw = SSD_R * SSD_HD
    nrow = nseq * SSD_G
    gmask = (lax.broadcasted_iota(jnp.int32, (SSD_G, SSD_DI), 1) // gw
             == lax.broadcasted_iota(jnp.int32, (SSD_G, SSD_DI), 0))
    pieces = [jnp.where(gmask, jnp.broadcast_to(xdt_ref[i:i + 1, :], (SSD_G, SSD_DI)), 0.0)
              for i in range(nseq)]
    pieces.append(jnp.zeros((LANES - nrow, SSD_DI), F32))
    a_t = jnp.concatenate(pieces, axis=0).T.astype(BF16)
    bm = jnp.concatenate([bm_ref[...], jnp.zeros((LANES - nrow, SSD_N), F32)], axis=0)
    rid = lax.broadcasted_iota(jnp.int32, (LANES, 1), 0) // SSD_G
    y_rows = []
    for i in range(nseq):
        outer = _dot(a_t, jnp.where(rid == i, bm, 0.0).astype(BF16))
        c_i = cm_ref[i * SSD_G:(i + 1) * SSD_G, :].astype(BF16)
        y_parts = []
        for g in range(SSD_G):
            hn = []
            for r in range(SSD_R):
                h = g * SSD_R + r
                hs = slice(h * SSD_HD, (h + 1) * SSD_HD)
                d = dec_ref[(blk * nseq + i) * SSD_H + h]
                v = h_ref[i, hs, :] * d + outer[hs, :]
                hout_ref[i, hs, :] = v
                hn.append(v)
            hg = jnp.concatenate(hn, axis=0).astype(BF16)
            y_parts.append(_dot_nt(c_i, hg)[g:g + 1, :])
        y_rows.append(jnp.concatenate(y_parts, axis=1))
    y = jnp.concatenate(y_rows, axis=0) + dskip_ref[...] * xsa_ref[...]
    y = y * _silu(z_ref[...])
    ms = jnp.mean(y * y, axis=-1, keepdims=True)
    yn_ref[...] = y * lax.rsqrt(ms + EPS) * normw_ref[...]


def _ssd_decode(xs, state, conv_state, cs, name):
    db = xs.shape[0]
    rb = _pick(db, (DEC_ROWS, 8))
    proj = _matmul(xs, cs["main"], name + "_in")
    const = lambda i: (0, 0)
    nst = (CONV_W - 1) * SSD_CONV
    newst, xsa, xdt, bca, dec = pl.pallas_call(
        _ssd_dec_pre_kernel,
        out_shape=(jax.ShapeDtypeStruct((db, nst), F32),
                   jax.ShapeDtypeStruct((db, SSD_DI), F32),
                   jax.ShapeDtypeStruct((db, SSD_DI), F32),
                   jax.ShapeDtypeStruct((db, 2 * SSD_G * SSD_N), F32),
                   jax.ShapeDtypeStruct((db, LANES), F32)),
        grid=(db // rb,),
        in_specs=[pl.BlockSpec((rb, SSD_DI), lambda i: (i, 1)),
                  pl.BlockSpec((rb, SSD_DI), lambda i: (i, 2)),
                  pl.BlockSpec((rb, D_MODEL), lambda i: (i, 0)),
                  pl.BlockSpec((rb, nst), lambda i: (i, 0)),
                  pl.BlockSpec((D_MODEL, LANES), const),
                  pl.BlockSpec((CONV_W, SSD_CONV), const),
                  pl.BlockSpec((1, SSD_CONV), const),
                  pl.BlockSpec((1, LANES), const),
                  pl.BlockSpec((1, LANES), const),
                  pl.BlockSpec((LANES, SSD_DI), const)],
        out_specs=(pl.BlockSpec((rb, nst), lambda i: (i, 0)),
                   pl.BlockSpec((rb, SSD_DI), lambda i: (i, 0)),
                   pl.BlockSpec((rb, SSD_DI), lambda i: (i, 0)),
                   pl.BlockSpec((rb, 2 * SSD_G * SSD_N), lambda i: (i, 0)),
                   pl.BlockSpec((rb, LANES), lambda i: (i, 0))),
        compiler_params=_cparams(("parallel",)),
        name=name + "_pre",
    )(proj, proj, xs, conv_state.reshape(db, nst), cs["wdt"], cs["conv_w"], cs["conv_b"],
      cs["dtb"], cs["aneg"], cs["expand"])
    ns = DEC_SEQS
    bm = bca[:, :SSD_G * SSD_N].reshape(db * SSD_G, SSD_N)
    cm = bca[:, SSD_G * SSD_N:].reshape(db * SSD_G, SSD_N)
    hout, yn = pl.pallas_call(
        _ssd_dec_state_kernel,
        out_shape=(jax.ShapeDtypeStruct((db, SSD_DI, SSD_N), F32),
                   jax.ShapeDtypeStruct((db, SSD_DI), F32)),
        grid=(db // ns,),
        in_specs=[pl.BlockSpec(memory_space=pltpu.SMEM),
                  pl.BlockSpec((ns, SSD_DI, SSD_N), lambda i: (i, 0, 0)),
                  pl.BlockSpec((ns, SSD_DI), lambda i: (i, 0)),
                  pl.BlockSpec((ns * SSD_G, SSD_N), lambda i: (i, 0)),
                  pl.BlockSpec((ns * SSD_G, SSD_N), lambda i: (i, 0)),
                  pl.BlockSpec((ns, SSD_DI), lambda i: (i, 0)),
                  pl.BlockSpec((ns, SSD_DI), lambda i: (i, 0)),
                  pl.BlockSpec((1, SSD_DI), const),
                  pl.BlockSpec((1, SSD_DI), const)],
        out_specs=(pl.BlockSpec((ns, SSD_DI, SSD_N), lambda i: (i, 0, 0)),
                   pl.BlockSpec((ns, SSD_DI), lambda i: (i, 0))),
        compiler_params=_cparams(("parallel",)),
        name=name + "_state",
    )(dec[:, :SSD_H].reshape(db * SSD_H), state.reshape(db, SSD_DI, SSD_N), xdt, bm, cm, xsa,
      proj, cs["dskip_e"], cs["norm_w"])
    return (yn, hout.reshape(db, SSD_H, SSD_HD, SSD_N),
            newst.reshape(db, CONV_W - 1, SSD_CONV))


def _gdn_dec_pre_kernel(q_ref, k_ref, v_ref, x_ref, st_ref, wa_ref, wb_ref, convw_ref, dtb_ref,
                        aneg_ref, newst_ref, qn_ref, kn_ref, va_ref, eg_ref, beta_ref, qk_ref):
    conv = _dec_conv((q_ref, k_ref, v_ref), st_ref, convw_ref, newst_ref)
    qkv = _silu(conv)
    xb = x_ref[...].astype(BF16)
    eg_ref[...] = jnp.exp(aneg_ref[...] * _softplus(_dot(xb, wa_ref[...]) + dtb_ref[...]))
    beta_ref[...] = _sigmoid(_dot(xb, wb_ref[...]))
    lane = lax.broadcasted_iota(jnp.int32, (1, LANES), 1)
    qk = jnp.zeros(qk_ref.shape, F32)
    for h in range(GDN_H):
        hs = slice(h * GDN_DK, (h + 1) * GDN_DK)
        q = qkv[:, hs]
        k = qkv[:, GDN_QK + h * GDN_DK:GDN_QK + (h + 1) * GDN_DK]
        q = q * lax.rsqrt(jnp.sum(q * q, axis=-1, keepdims=True) + 1e-6) * (GDN_DK ** -0.5)
        k = k * lax.rsqrt(jnp.sum(k * k, axis=-1, keepdims=True) + 1e-6)
        qn_ref[:, hs] = q
        kn_ref[:, hs] = k
        qk = jnp.where(lane == h, jnp.sum(q * k, axis=-1, keepdims=True), qk)
    qk_ref[...] = qk
    va_ref[...] = qkv[:, 2 * GDN_QK:]


def _gdn_dec_state_kernel(eg_ref, beta_ref, qk_ref, s_ref, q_ref, k_ref, v_ref, z_ref, normw_ref,
                          sout_ref, on_ref):
    blk = pl.program_id(0)
    nseq = s_ref.shape[0]
    nrow = nseq * GDN_H
    kparts = [k_ref[:, h * GDN_DK:(h + 1) * GDN_DK] for h in range(GDN_H)]
    kparts.append(jnp.zeros((LANES - nrow, GDN_DK), F32))
    k_t = jnp.concatenate(kparts, axis=0).T.astype(BF16)
    rid = lax.broadcasted_iota(jnp.int32, (LANES, 1), 0)
    o_cols = []
    for h in range(GDN_H):
        hs = slice(h * GDN_DK, (h + 1) * GDN_DK)
        vs = slice(h * GDN_DV, (h + 1) * GDN_DV)
        kq = jnp.concatenate([k_ref[:, hs], q_ref[:, hs]], axis=0).astype(BF16)
        o_rows = []
        for i in range(nseq):
            sidx = (blk * nseq + i) * GDN_H + h
            eg = eg_ref[sidx]
            s = s_ref[i, h]
            r = _dot(kq, s.astype(BF16))
            v_new = beta_ref[sidx] * (v_ref[i:i + 1, vs] - eg * r[i:i + 1, :])
            o_rows.append(eg * r[nseq + i:nseq + i + 1, :] + qk_ref[sidx] * v_new)
            rm = jnp.where(rid == h * nseq + i, jnp.broadcast_to(v_new, (LANES, GDN_DV)), 0.0)
            sout_ref[i, h] = s * eg + _dot(k_t, rm.astype(BF16))
        o = jnp.concatenate(o_rows, axis=0)
        ms = jnp.mean(o * o, axis=-1, keepdims=True)
        o_cols.append(o * lax.rsqrt(ms + EPS) * normw_ref[...] * _silu(z_ref[:, vs]))
    on_ref[...] = jnp.concatenate(o_cols, axis=1)


def _gdn_decode(xs, state, conv_state, cs, name):
    db = xs.shape[0]
    rb = _pick(db, (DEC_ROWS, 8))
    proj = _matmul(xs, cs["main"], name + "_in")
    const = lambda i: (0, 0)
    nst = (CONV_W - 1) * GDN_CONV
    row = lambda w: pl.BlockSpec((rb, w), lambda i: (i, 0))
    newst, qn, kn, va, eg, beta, qk = pl.pallas_call(
        _gdn_dec_pre_kernel,
        out_shape=(jax.ShapeDtypeStruct((db, nst), F32),
                   jax.ShapeDtypeStruct((db, GDN_QK), F32),
                   jax.ShapeDtypeStruct((db, GDN_QK), F32),
                   jax.ShapeDtypeStruct((db, GDN_VW), F32),
                   jax.ShapeDtypeStruct((db, LANES), F32),
                   jax.ShapeDtypeStruct((db, LANES), F32),
                   jax.ShapeDtypeStruct((db, LANES), F32)),
        grid=(db // rb,),
        in_specs=[pl.BlockSpec((rb, GDN_QK), lambda i: (i, 0)),
                  pl.BlockSpec((rb, GDN_QK), lambda i: (i, 1)),
                  pl.BlockSpec((rb, GDN_VW), lambda i: (i, 1)),
                  row(D_MODEL), row(nst),
                  pl.BlockSpec((D_MODEL, LANES), const),
                  pl.BlockSpec((D_MODEL, LANES), const),
                  pl.BlockSpec((CONV_W, GDN_CONV), const),
                  pl.BlockSpec((1, LANES), const),
                  pl.BlockSpec((1, LANES), const)],
        out_specs=(row(nst), row(GDN_QK), row(GDN_QK), row(GDN_VW), row(LANES), row(LANES),
                   row(LANES)),
        compiler_params=_cparams(("parallel",)),
        name=name + "_pre",
    )(proj, proj, proj, xs, conv_state.reshape(db, nst), cs["wa"], cs["wb"], cs["conv_w"],
      cs["dtb"], cs["aneg"])
    ns = DEC_SEQS
    smem = pl.BlockSpec(memory_space=pltpu.SMEM)
    flat = lambda a: a[:, :GDN_H].reshape(db * GDN_H)
    sout, on = pl.pallas_call(
        _gdn_dec_state_kernel,
        out_shape=(jax.ShapeDtypeStruct((db, GDN_H, GDN_DK, GDN_DV), F32),
                   jax.ShapeDtypeStruct((db, GDN_VW), F32)),
        grid=(db // ns,),
        in_specs=[smem, smem, smem,
                  pl.BlockSpec((ns, GDN_H, GDN_DK, GDN_DV), lambda i: (i, 0, 0, 0)),
                  pl.BlockSpec((ns, GDN_QK), lambda i: (i, 0)),
                  pl.BlockSpec((ns, GDN_QK), lambda i: (i, 0)),
                  pl.BlockSpec((ns, GDN_VW), lambda i: (i, 0)),
                  pl.BlockSpec((ns, GDN_VW), lambda i: (i, 2)),
                  pl.BlockSpec((1, GDN_DV), const)],
        out_specs=(pl.BlockSpec((ns, GDN_H, GDN_DK, GDN_DV), lambda i: (i, 0, 0, 0)),
                   pl.BlockSpec((ns, GDN_VW), lambda i: (i, 0))),
        compiler_params=_cparams(("parallel",)),
        name=name + "_state",
    )(flat(eg), flat(beta), flat(qk), state, qn, kn, va, proj, cs["norm_w"])
    return on, sout, newst.reshape(db, CONV_W - 1, GDN_CONV)


DEC_PAGES = 4


def _sba_dec_kernel(pt_ref, q_ref, zg_ref, bias_ref, esum_ref, eexp_ref, *rest):
    npg = DEC_PAGES
    k_refs = rest[:npg]
    v_refs = rest[npg:2 * npg]
    y_ref, acc_ref, car_ref = rest[2 * npg:]
    s = pl.program_id(1)
    ps = k_refs[0].shape[1]

    @pl.when(s == 0)
    def _():
        acc_ref[...] = jnp.zeros(acc_ref.shape, F32)
        car_ref[...] = jnp.zeros(car_ref.shape, F32)

    q = q_ref[0]
    r = lax.broadcasted_iota(jnp.int32, (ps, ps), 0)
    c = lax.broadcasted_iota(jnp.int32, (ps, ps), 1)
    later = jnp.where(c > r, 1.0, 0.0).astype(BF16)
    for p in range(npg):
        z = _dot_x01(k_refs[p][0] * q, esum_ref[...], terms=2) * (SBA_HD ** -0.5) + bias_ref[...]
        sp = _softplus(z)
        aft = _dot_01x(later, -sp, terms=2) + car_ref[...]
        a = jnp.exp(z - sp + aft)
        acc_ref[...] += _dot_x01(a, eexp_ref[...], terms=2) * v_refs[p][0]
        car_ref[...] -= jnp.sum(sp, axis=0, keepdims=True)

    @pl.when(s == pl.num_programs(1) - 1)
    def _():
        o = jnp.sum(acc_ref[...], axis=0, keepdims=True)
        y_ref[0] = o * _silu(zg_ref[0])


def _sba_decode(xs, cache_k, cache_v, page_table, in_w, sb_bias, name):
    db = xs.shape[0]
    n_pool, ps = cache_k.shape[0], cache_k.shape[1]
    npages = page_table.shape[1]
    npg = DEC_PAGES
    assert npages % npg == 0
    proj = _matmul(xs, in_w, name + "_in")
    q = proj[:, :SBA_W].reshape(db, 1, SBA_W)
    zg = proj[:, 3 * SBA_W:].reshape(db, 1, SBA_W)
    bias = jnp.pad(sb_bias.astype(F32), (0, LANES - SBA_H)).reshape(1, LANES)
    hh = lax.broadcasted_iota(jnp.int32, (LANES, SBA_W), 0)
    cc = lax.broadcasted_iota(jnp.int32, (LANES, SBA_W), 1) // SBA_HD
    eexp = (hh == cc).astype(BF16)
    esum = eexp.T
    kpool = cache_k.reshape(n_pool, ps, SBA_W)
    vpool = cache_v.reshape(n_pool, ps, SBA_W)

    def page_spec(r):
        return pl.BlockSpec((1, ps, SBA_W),
                            lambda b, s, pt: (pt[b * npages + npages - 1 - (s * npg + r)], 0, 0))

    one = pl.BlockSpec((1, 1, SBA_W), lambda b, s, pt: (b, 0, 0))
    const = lambda b, s, pt: (0, 0)
    y = pl.pallas_call(
        _sba_dec_kernel,
        out_shape=jax.ShapeDtypeStruct((db, 1, SBA_W), F32),
        grid_spec=pltpu.PrefetchScalarGridSpec(
            num_scalar_prefetch=1,
            grid=(db, npages // npg),
            in_specs=[one, one,
                      pl.BlockSpec((1, LANES), const),
                      pl.BlockSpec((SBA_W, LANES), const),
                      pl.BlockSpec((LANES, SBA_W), const)]
                     + [page_spec(r) for r in range(npg)] + [page_spec(r) for r in range(npg)],
            out_specs=one,
            scratch_shapes=[pltpu.VMEM((ps, SBA_W), F32), pltpu.VMEM((1, LANES), F32)]),
        compiler_params=_cparams(("parallel", "arbitrary")),
        name=name,
    )(page_table.reshape(-1).astype(jnp.int32), q, zg, bias, esum, eexp,
      *([kpool] * npg), *([vpool] * npg))
    return y.reshape(db, SBA_W), proj


def _pad_prompt(x_prompt, meta_tokens):
    nb, seq, d = x_prompt.shape
    n_meta = meta_tokens.shape[0]
    real = n_meta + seq
    pad = (-real) % CHUNK
    lp = pad + real
    meta = jnp.broadcast_to(meta_tokens[None].astype(x_prompt.dtype), (nb, n_meta, d))
    xp = jnp.concatenate([jnp.zeros((nb, pad, d), x_prompt.dtype), meta, x_prompt], axis=1)
    return xp.reshape(nb * lp, d), pad, lp


def kernel(x_prompt, x_sample, state_l0_ssm, state_l0_conv, state_l1_delta, state_l1_conv, cache_l2_k, cache_l2_v, state_l3_ssm, state_l3_conv, page_table, meta_tokens, l0_in_w, l0_conv_w, l0_conv_b, l0_dt_bias, l0_a_log, l0_d_skip, l0_norm_w, l0_out_w, l0_ln_g, l0_ln_b, l1_in_w, l1_conv_w, l1_dt_bias, l1_a_log, l1_norm_w, l1_out_w, l1_ln_g, l1_ln_b, l2_in_w, l2_sb_bias, l2_out_w, l2_ln_g, l2_ln_b, l3_in_w, l3_conv_w, l3_conv_b, l3_dt_bias, l3_a_log, l3_d_skip, l3_norm_w, l3_out_w, l3_ln_g, l3_ln_b):
    nb, seq, d = x_prompt.shape
    db = x_sample.shape[0]
    assert x_sample.shape[1] == 1 and d == D_MODEL
    xp, pad, lp = _pad_prompt(x_prompt, meta_tokens)
    real = lp - pad
    xs = x_sample.reshape(db, d)

    def ssd_layer(xp, xs, w, ln_g, ln_b, state, conv_state, name):
        cs = _ssd_consts(w)
        yn, p_ssm, p_conv = _ssd_prompt(xp, cs, pad, nb, lp, name + "_prompt")
        xp = _out_ln(yn, cs["out_w"], xp, ln_g, ln_b, name + "_prompt_out")
        yd, s_ssm, s_conv = _ssd_decode(xs, state, conv_state, cs, name + "_dec")
        xs = _out_ln(yd, cs["out_w"], xs, ln_g, ln_b, name + "_dec_out")
        return xp, xs, p_ssm, p_conv, s_ssm, s_conv.reshape(db, CONV_W - 1, SSD_CONV)

    xp, xs, p_l0_ssm, p_l0_conv, s_l0_ssm, s_l0_conv = ssd_layer(
        xp, xs, (l0_in_w, l0_conv_w, l0_conv_b, l0_dt_bias, l0_a_log, l0_d_skip, l0_norm_w, l0_out_w),
        l0_ln_g, l0_ln_b, state_l0_ssm, state_l0_conv, "l0")

    cs1 = _gdn_consts((l1_in_w, l1_conv_w, l1_dt_bias, l1_a_log, l1_norm_w, l1_out_w))
    on, p_l1_delta, p_l1_conv = _gdn_prompt(xp, cs1, pad, nb, lp, "l1_prompt")
    xp = _out_ln(on, cs1["out_w"], xp, l1_ln_g, l1_ln_b, "l1_prompt_out")
    od, s_l1_delta, s_l1_conv = _gdn_decode(xs, state_l1_delta, state_l1_conv, cs1, "l1_dec")
    xs = _out_ln(od, cs1["out_w"], xs, l1_ln_g, l1_ln_b, "l1_dec_out")

    w2_in = l2_in_w.astype(BF16)
    w2_out = l2_out_w.astype(BF16)
    ya, proj_p = _sba_prompt(xp, w2_in, l2_sb_bias, pad, nb, lp, "l2_prompt")
    xp = _out_ln(ya, w2_out, xp, l2_ln_g, l2_ln_b, "l2_prompt_out")
    proj_p = proj_p.reshape(nb, lp, 4 * SBA_W)[:, pad:]
    p_l2_k = proj_p[..., SBA_W:2 * SBA_W].reshape(nb, real, SBA_H, SBA_HD)
    p_l2_v = proj_p[..., 2 * SBA_W:3 * SBA_W].reshape(nb, real, SBA_H, SBA_HD)
    yd, proj_s = _sba_decode(xs, cache_l2_k, cache_l2_v, page_table, w2_in, l2_sb_bias, "l2_dec")
    xs = _out_ln(yd, w2_out, xs, l2_ln_g, l2_ln_b, "l2_dec_out")
    s_l2_k = proj_s[:, SBA_W:2 * SBA_W].reshape(db, 1, SBA_H, SBA_HD)
    s_l2_v = proj_s[:, 2 * SBA_W:3 * SBA_W].reshape(db, 1, SBA_H, SBA_HD)

    xp, xs, p_l3_ssm, p_l3_conv, s_l3_ssm, s_l3_conv = ssd_layer(
        xp, xs, (l3_in_w, l3_conv_w, l3_conv_b, l3_dt_bias, l3_a_log, l3_d_skip, l3_norm_w, l3_out_w),
        l3_ln_g, l3_ln_b, state_l3_ssm, state_l3_conv, "l3")

    y_prompt = xp.reshape(nb, lp, d)[:, pad + meta_tokens.shape[0]:]
    y_sample = xs.reshape(db, 1, d)
    return (y_prompt, y_sample,
            p_l0_ssm, p_l0_conv, p_l1_delta, p_l1_conv, p_l2_k, p_l2_v, p_l3_ssm, p_l3_conv,
            s_l0_ssm, s_l0_conv, s_l1_delta, s_l1_conv, s_l2_k, s_l2_v, s_l3_ssm, s_l3_conv)
```

```python
import functools
import math

import jax
import jax.numpy as jnp
from jax import lax
from jax.experimental import pallas as pl
from jax.experimental.pallas import tpu as pltpu

F32 = jnp.float32
BF16 = jnp.bfloat16

N_META = 16
CONV_W = 4
EPS = 1e-5
D_MODEL = 1024
SSD_DI = 2048
SSD_HD = 64
SSD_H = 32
SSD_G = 8
SSD_N = 128
SSD_R = SSD_H // SSD_G
SSD_CONV = SSD_DI + 2 * SSD_G * SSD_N
GDN_H = 8
GDN_DK = 128
GDN_DV = 256
GDN_QK = GDN_H * GDN_DK
GDN_VW = GDN_H * GDN_DV
GDN_CONV = 2 * GDN_QK + GDN_VW
SBA_H = 16
SBA_HD = 64
SBA_W = SBA_H * SBA_HD
DEPTH = 4
DN_ALPHA = (2 * DEPTH) ** 0.25

LANES = 128
LOG2E = 1.4426950408889634
CHUNK = 128
VMEM_LIMIT = 56 * 1024 * 1024


def _cparams(sem):
    return pltpu.CompilerParams(dimension_semantics=sem, vmem_limit_bytes=VMEM_LIMIT)


def _dot(a, b):
    return jnp.dot(a, b, preferred_element_type=F32)


def _dot_nt(a, b):
    return lax.dot_general(a, b, (((1,), (1,)), ((), ())), preferred_element_type=F32)


def _split(x, terms):
    parts = []
    r = x
    for i in range(terms):
        p = r.astype(BF16)
        parts.append(p)
        if i + 1 < terms:
            r = r - p.astype(F32)
    return parts


def _dot_x01(x, e, terms=3):
    m = x.shape[0]
    parts = _split(x, terms)
    r = _dot(jnp.concatenate(parts, axis=0), e)
    out = r[0:m]
    for i in range(1, terms):
        out = out + r[i * m:(i + 1) * m]
    return out


def _dot_01x(e, x, terms=3):
    n = x.shape[1]
    parts = _split(x, terms)
    r = _dot(e, jnp.concatenate(parts, axis=1))
    out = r[:, 0:n]
    for i in range(1, terms):
        out = out + r[:, i * n:(i + 1) * n]
    return out


def _sigmoid(x):
    return 1.0 / (1.0 + jnp.exp(-x))


def _silu(x):
    return x * _sigmoid(x)


def _softplus(x):
    return jnp.maximum(x, 0.0) + jnp.log1p(jnp.exp(-jnp.abs(x)))


def _tri(n, strict):
    r = lax.broadcasted_iota(jnp.int32, (n, n), 0)
    c = lax.broadcasted_iota(jnp.int32, (n, n), 1)
    return (c < r) if strict else (c <= r)


def _mm_kernel(x_ref, w_ref, o_ref, xb_ref):
    @pl.when(pl.program_id(1) == 0)
    def _():
        xb_ref[...] = x_ref[...].astype(BF16)

    o_ref[...] = _dot(xb_ref[...], w_ref[...]).astype(o_ref.dtype)


def _pick(n, cands):
    for c in cands:
        if n % c == 0:
            return c
    return n


def _matmul(x, w, name):
    m, k = x.shape
    n = w.shape[1]
    tm = _pick(m, (1536, 1024, 768, 512, 384, 256, 128))
    tn = _pick(n, (1024, 512, 256, 128))
    return pl.pallas_call(
        _mm_kernel,
        out_shape=jax.ShapeDtypeStruct((m, n), F32),
        grid=(m // tm, n // tn),
        in_specs=[pl.BlockSpec((tm, k), lambda i, j: (i, 0)),
                  pl.BlockSpec((k, tn), lambda i, j: (0, j))],
        out_specs=pl.BlockSpec((tm, tn), lambda i, j: (i, j)),
        scratch_shapes=[pltpu.VMEM((tm, k), BF16)],
        compiler_params=_cparams(("parallel", "arbitrary")),
        name=name,
    )(x, w)


def _out_ln_kernel(y_ref, w_ref, x_ref, g_ref, b_ref, o_ref):
    f = _dot(y_ref[...].astype(BF16), w_ref[...])
    r = DN_ALPHA * x_ref[...] + f
    mu = jnp.mean(r, axis=-1, keepdims=True)
    d = r - mu
    var = jnp.mean(d * d, axis=-1, keepdims=True)
    o_ref[...] = d * lax.rsqrt(var + EPS) * g_ref[...] + b_ref[...]


def _out_ln(y, w, x, g, b, name):
    m, k = y.shape
    d = w.shape[1]
    tm = _pick(m, (512, 384, 256, 128))
    return pl.pallas_call(
        _out_ln_kernel,
        out_shape=jax.ShapeDtypeStruct((m, d), F32),
        grid=(m // tm,),
        in_specs=[pl.BlockSpec((tm, k), lambda i: (i, 0)),
                  pl.BlockSpec((k, d), lambda i: (0, 0)),
                  pl.BlockSpec((tm, d), lambda i: (i, 0)),
                  pl.BlockSpec((1, d), lambda i: (0, 0)),
                  pl.BlockSpec((1, d), lambda i: (0, 0))],
        out_specs=pl.BlockSpec((tm, d), lambda i: (i, 0)),
        compiler_params=_cparams(("parallel",)),
        name=name,
    )(y, w, x, g.reshape(1, d), b.reshape(1, d))


def _conv_chunk(ext_ref, parts, valid, c, t, convw_ref, tail_ref, is_last):
    ncol = ext_ref.shape[1]

    @pl.when(c == 0)
    def _():
        ext_ref[0:8, :] = jnp.zeros((8, ncol), F32)

    @pl.when(c > 0)
    def _():
        ext_ref[0:8, :] = ext_ref[t:t + 8, :]

    off = 0
    for p in parts:
        w = p.shape[1]
        ext_ref[8:8 + t, off:off + w] = jnp.where(valid, p[...], 0.0)
        off += w

    @pl.when(is_last)
    def _():
        tail_ref[0] = ext_ref[t + 5:t + 8, :]

    acc = convw_ref[0:1, :] * ext_ref[5:5 + t, :]
    for j in range(1, CONV_W):
        acc = acc + convw_ref[j:j + 1, :] * ext_ref[5 + j:5 + j + t, :]
    return acc


def _ssd_prompt_kernel(pad, z_ref, xs_ref, bc_ref, x_ref, wdt_ref, convw_ref, convb_ref,
                       dtb_ref, aneg_ref, dskip_ref, normw_ref, expand_ref,
                       yn_ref, hfin_ref, tail_ref, ext_ref, ht_ref, y_ref):
    c = pl.program_id(1)
    is_last = c == pl.num_programs(1) - 1
    t = z_ref.shape[0]
    gw = SSD_R * SSD_HD

    @pl.when(c == 0)
    def _():
        ht_ref[...] = jnp.zeros(ht_ref.shape, F32)

    row = lax.broadcasted_iota(jnp.int32, (t, 1), 0)
    valid = jnp.logical_or(c > 0, row >= pad)

    conv = _conv_chunk(ext_ref, (xs_ref, bc_ref), valid, c, t, convw_ref, tail_ref, is_last)
    xbc = _silu(conv + convb_ref[...])
    xs = xbc[:, :SSD_DI]
    bm = xbc[:, SSD_DI:SSD_DI + SSD_G * SSD_N]
    cm = xbc[:, SSD_DI + SSD_G * SSD_N:]

    dt_raw = _dot(x_ref[...].astype(BF16), wdt_ref[...])
    dt = jnp.where(valid, _softplus(dt_raw + dtb_ref[...]), 0.0)
    a = dt * aneg_ref[...]
    ltri = jnp.where(_tri(t, False), 1.0, 0.0).astype(BF16)
    acum = _dot_01x(ltri, a)
    acum_t = acum.T

    both = _dot_x01(jnp.concatenate([dt, acum], axis=0), expand_ref[...])
    dt_e = both[0:t]
    acum_e = both[t:2 * t]
    last_e = acum_e[t - 1:t, :]
    eac_e = jnp.exp(acum_e)
    dte_e = jnp.exp(last_e - acum_e)
    cd_e = jnp.exp(last_e)

    xdt = xs * dt_e
    xw = xdt * dte_e
    tri = _tri(t, False)
    lane_g = lax.broadcasted_iota(jnp.int32, (1, gw), 1) // SSD_HD

    for g in range(SSD_G):
        gs = slice(g * gw, (g + 1) * gw)
        ns = slice(g * SSD_N, (g + 1) * SSD_N)
        bg = bm[:, ns]
        cg = cm[:, ns].astype(BF16)
        cb = _dot_nt(cg, bg.astype(BF16))
        xdt_g = xdt[:, gs].astype(BF16)
        ht = ht_ref[g]
        y_g = _dot(cg, ht.astype(BF16)) * eac_e[:, gs]
        for r in range(SSD_R):
            h = g * SSD_R + r
            diff = acum[:, h:h + 1] - acum_t[h:h + 1, :]
            seg = jnp.exp(jnp.where(tri, diff, -jnp.inf))
            yd = _dot((cb * seg).astype(BF16), xdt_g)
            y_g = y_g + jnp.where(lane_g == r, yd, 0.0)
        y_g = y_g + dskip_ref[:, gs] * xs[:, gs]
        y_ref[:, gs] = y_g
        ht_ref[g] = ht * cd_e[:, gs] + _dot(bg.T.astype(BF16), xw[:, gs].astype(BF16))

    y = y_ref[...] * _silu(z_ref[...])
    ms = jnp.mean(y * y, axis=-1, keepdims=True)
    yn_ref[...] = (y * lax.rsqrt(ms + EPS) * normw_ref[...]).astype(yn_ref.dtype)

    @pl.when(is_last)
    def _():
        for g in range(SSD_G):
            hfin_ref[0, g * gw:(g + 1) * gw, :] = ht_ref[g].T


def _ssd_consts(w):
    in_w, conv_w, conv_b, dt_bias, a_log, d_skip, norm_w, out_w = w
    main = in_w[:, :SSD_DI + SSD_CONV].astype(BF16)
    wdt = jnp.pad(in_w[:, SSD_DI + SSD_CONV:], ((0, 0), (0, LANES - SSD_H))).astype(BF16)
    dtb = jnp.pad(dt_bias.astype(F32), (0, LANES - SSD_H)).reshape(1, LANES)
    aneg = jnp.pad(-jnp.exp(a_log.astype(F32)), (0, LANES - SSD_H)).reshape(1, LANES)
    dskip_e = jnp.repeat(d_skip.astype(F32), SSD_HD).reshape(1, SSD_DI)
    hh = lax.broadcasted_iota(jnp.int32, (LANES, SSD_DI), 0)
    cc = lax.broadcasted_iota(jnp.int32, (LANES, SSD_DI), 1) // SSD_HD
    expand = (hh == cc).astype(BF16)
    return dict(main=main, wdt=wdt, dtb=dtb, aneg=aneg, dskip_e=dskip_e, expand=expand,
                conv_w=conv_w.astype(F32), conv_b=conv_b.astype(F32).reshape(1, SSD_CONV),
                norm_w=norm_w.astype(F32).reshape(1, SSD_DI), out_w=out_w.astype(BF16))


def _ssd_prompt(xp, cs, pad, nb, lp, name):
    t = CHUNK
    nc = lp // t
    proj = _matmul(xp, cs["main"], name + "_in")
    const = lambda b, c: (0, 0)
    rows = lambda b, c: (b * nc + c, 0)
    yn, hfin, tail = pl.pallas_call(
        functools.partial(_ssd_prompt_kernel, pad),
        out_shape=(jax.ShapeDtypeStruct((nb * lp, SSD_DI), BF16),
                   jax.ShapeDtypeStruct((nb, SSD_DI, SSD_N), F32),
                   jax.ShapeDtypeStruct((nb, CONV_W - 1, SSD_CONV), F32)),
        grid=(nb, nc),
        in_specs=[pl.BlockSpec((t, SSD_DI), lambda b, c: (b * nc + c, 0)),
                  pl.BlockSpec((t, SSD_DI), lambda b, c: (b * nc + c, 1)),
                  pl.BlockSpec((t, SSD_DI), lambda b, c: (b * nc + c, 2)),
                  pl.BlockSpec((t, D_MODEL), rows),
                  pl.BlockSpec((D_MODEL, LANES), const),
                  pl.BlockSpec((CONV_W, SSD_CONV), const),
                  pl.BlockSpec((1, SSD_CONV), const),
                  pl.BlockSpec((1, LANES), const),
                  pl.BlockSpec((1, LANES), const),
                  pl.BlockSpec((1, SSD_DI), const),
                  pl.BlockSpec((1, SSD_DI), const),
                  pl.BlockSpec((LANES, SSD_DI), const)],
        out_specs=(pl.BlockSpec((t, SSD_DI), rows),
                   pl.BlockSpec((1, SSD_DI, SSD_N), lambda b, c: (b, 0, 0)),
                   pl.BlockSpec((1, CONV_W - 1, SSD_CONV), lambda b, c: (b, 0, 0))),
        scratch_shapes=[pltpu.VMEM((t + 8, SSD_CONV), F32),
                        pltpu.VMEM((SSD_G, SSD_N, SSD_R * SSD_HD), F32),
                        pltpu.VMEM((t, SSD_DI), F32)],
        compiler_params=_cparams(("parallel", "arbitrary")),
        name=name,
    )(proj, proj, proj, xp, cs["wdt"], cs["conv_w"], cs["conv_b"], cs["dtb"], cs["aneg"],
      cs["dskip_e"], cs["norm_w"], cs["expand"])
    return yn, hfin.reshape(nb, SSD_H, SSD_HD, SSD_N), tail


INV_BLOCK = 16


def _unit_lower_inverse(mats, t):
    n = range(len(mats))
    r = lax.broadcasted_iota(jnp.int32, (t, t), 0)
    c = lax.broadcasted_iota(jnp.int32, (t, t), 1)
    b = INV_BLOCK
    diag = (r // b) == (c // b)
    eye = jnp.where(r == c, 1.0, 0.0)
    ap = [jnp.where(diag, a, 0.0) for a in mats]
    p = [eye - ap[i] for i in n]
    for _ in range(int(math.log2(b)) - 1):
        apb = [ap[i].astype(BF16) for i in n]
        ap = [_dot(apb[i], apb[i]) for i in n]
        prod = [_dot(p[i].astype(BF16), ap[i].astype(BF16)) for i in n]
        p = [p[i] + prod[i] for i in n]
    while b < t:
        sel = jnp.logical_and((r // (2 * b)) == (c // (2 * b)), (r // b) != (c // b))
        off = [jnp.where(sel, a, 0.0).astype(BF16) for a in mats]
        pb = [p[i].astype(BF16) for i in n]
        left = [_dot(pb[i], off[i]).astype(BF16) for i in n]
        corr = [_dot(left[i], pb[i]) for i in n]
        p = [p[i] - corr[i] for i in n]
        b *= 2
    return p


def _gdn_prompt_kernel(pad, q_ref, k_ref, v_ref, z_ref, x_ref, wa_ref, wb_ref, convw_ref,
                       dtb_ref, aneg_ref, normw_ref,
                       on_ref, sfin_ref, tail_ref, ext_ref, s_ref):
    c = pl.program_id(1)
    is_last = c == pl.num_programs(1) - 1
    t = q_ref.shape[0]

    @pl.when(c == 0)
    def _():
        s_ref[...] = jnp.zeros(s_ref.shape, F32)

    row = lax.broadcasted_iota(jnp.int32, (t, 1), 0)
    valid = jnp.logical_or(c > 0, row >= pad)

    conv = _conv_chunk(ext_ref, (q_ref, k_ref, v_ref), valid, c, t, convw_ref, tail_ref, is_last)
    qkv = _silu(conv)

    xb = x_ref[...].astype(BF16)
    a_raw = _dot(xb, wa_ref[...])
    b_raw = _dot(xb, wb_ref[...])
    gate = jnp.where(valid, aneg_ref[...] * _softplus(a_raw + dtb_ref[...]), 0.0)
    beta = jnp.where(valid, _sigmoid(b_raw), 0.0)
    ltri = jnp.where(_tri(t, False), 1.0, 0.0).astype(BF16)
    gc = _dot_01x(ltri, gate)
    gc_t = gc.T
    tri = _tri(t, False)
    strict = _tri(t, True)

    heads = range(GDN_H)
    q, k, v = [], [], []
    for h in heads:
        qh = qkv[:, h * GDN_DK:(h + 1) * GDN_DK]
        kh = qkv[:, GDN_QK + h * GDN_DK:GDN_QK + (h + 1) * GDN_DK]
        q.append(qh * lax.rsqrt(jnp.sum(qh * qh, axis=-1, keepdims=True) + 1e-6) * (GDN_DK ** -0.5))
        k.append(kh * lax.rsqrt(jnp.sum(kh * kh, axis=-1, keepdims=True) + 1e-6))
        v.append(qkv[:, 2 * GDN_QK + h * GDN_DV:2 * GDN_QK + (h + 1) * GDN_DV])
    gcol = [gc[:, h:h + 1] for h in heads]
    bcol = [beta[:, h:h + 1] for h in heads]
    dec = [jnp.exp(jnp.where(tri, gcol[h] - gc_t[h:h + 1, :], -jnp.inf)) for h in heads]
    eg = [jnp.exp(gcol[h]) for h in heads]
    kb = [k[h] * bcol[h] for h in heads]
    kbf = [k[h].astype(BF16) for h in heads]
    kk = [_dot_nt(kb[h].astype(BF16), kbf[h]) for h in heads]
    qk = [_dot_nt(q[h].astype(BF16), kbf[h]) for h in heads]
    a = [jnp.where(strict, kk[h] * dec[h], 0.0) for h in heads]
    qk = [(qk[h] * dec[h]).astype(BF16) for h in heads]
    tinv = [m.astype(BF16) for m in _unit_lower_inverse(a, t)]
    u = [_dot(tinv[h], (v[h] * bcol[h]).astype(BF16)) for h in heads]
    w = [_dot(tinv[h], (kb[h] * eg[h]).astype(BF16)) for h in heads]
    s = [s_ref[h] for h in heads]
    sb = [s[h].astype(BF16) for h in heads]
    ws = [_dot(w[h].astype(BF16), sb[h]) for h in heads]
    qs = [_dot((q[h] * eg[h]).astype(BF16), sb[h]) for h in heads]
    vnb = [(u[h] - ws[h]).astype(BF16) for h in heads]
    o = [qs[h] + _dot(qk[h], vnb[h]) for h in heads]
    gl = [gc[t - 1:t, h:h + 1] for h in heads]
    kd = [(k[h] * jnp.exp(gl[h] - gcol[h])).T.astype(BF16) for h in heads]
    upd = [_dot(kd[h], vnb[h]) for h in heads]
    for h in heads:
        s_ref[h] = s[h] * jnp.exp(gl[h]) + upd[h]
        ms = jnp.mean(o[h] * o[h], axis=-1, keepdims=True)
        zh = z_ref[:, h * GDN_DV:(h + 1) * GDN_DV]
        on_ref[:, h * GDN_DV:(h + 1) * GDN_DV] = (
            o[h] * lax.rsqrt(ms + EPS) * normw_ref[...] * _silu(zh)).astype(on_ref.dtype)

    @pl.when(is_last)
    def _():
        sfin_ref[0] = s_ref[...]


def _gdn_consts(w):
    in_w, conv_w, dt_bias, a_log, norm_w, out_w = w
    nmain = GDN_CONV + GDN_VW
    main = in_w[:, :nmain].astype(BF16)
    wa = jnp.pad(in_w[:, nmain:nmain + GDN_H], ((0, 0), (0, LANES - GDN_H))).astype(BF16)
    wb = jnp.pad(in_w[:, nmain + GDN_H:], ((0, 0), (0, LANES - GDN_H))).astype(BF16)
    dtb = jnp.pad(dt_bias.astype(F32), (0, LANES - GDN_H)).reshape(1, LANES)
    aneg = jnp.pad(-jnp.exp(a_log.astype(F32)), (0, LANES - GDN_H)).reshape(1, LANES)
    return dict(main=main, wa=wa, wb=wb, dtb=dtb, aneg=aneg, conv_w=conv_w.astype(F32),
                norm_w=norm_w.astype(F32).reshape(1, GDN_DV), out_w=out_w.astype(BF16))


def _gdn_prompt(xp, cs, pad, nb, lp, name):
    t = CHUNK
    nc = lp // t
    proj = _matmul(xp, cs["main"], name + "_in")
    const = lambda b, c: (0, 0)
    rows = lambda b, c: (b * nc + c, 0)
    on, sfin, tail = pl.pallas_call(
        functools.partial(_gdn_prompt_kernel, pad),
        out_shape=(jax.ShapeDtypeStruct((nb * lp, GDN_VW), BF16),
                   jax.ShapeDtypeStruct((nb, GDN_H, GDN_DK, GDN_DV), F32),
                   jax.ShapeDtypeStruct((nb, CONV_W - 1, GDN_CONV), F32)),
        grid=(nb, nc),
        in_specs=[pl.BlockSpec((t, GDN_QK), lambda b, c: (b * nc + c, 0)),
                  pl.BlockSpec((t, GDN_QK), lambda b, c: (b * nc + c, 1)),
                  pl.BlockSpec((t, GDN_VW), lambda b, c: (b * nc + c, 1)),
                  pl.BlockSpec((t, GDN_VW), lambda b, c: (b * nc + c, 2)),
                  pl.BlockSpec((t, D_MODEL), rows),
                  pl.BlockSpec((D_MODEL, LANES), const),
                  pl.BlockSpec((D_MODEL, LANES), const),
                  pl.BlockSpec((CONV_W, GDN_CONV), const),
                  pl.BlockSpec((1, LANES), const),
                  pl.BlockSpec((1, LANES), const),
                  pl.BlockSpec((1, GDN_DV), const)],
        out_specs=(pl.BlockSpec((t, GDN_VW), rows),
                   pl.BlockSpec((1, GDN_H, GDN_DK, GDN_DV), lambda b, c: (b, 0, 0, 0)),
                   pl.BlockSpec((1, CONV_W - 1, GDN_CONV), lambda b, c: (b, 0, 0))),
        scratch_shapes=[pltpu.VMEM((t + 8, GDN_CONV), F32),
                        pltpu.VMEM((GDN_H, GDN_DK, GDN_DV), F32)],
        compiler_params=_cparams(("parallel", "arbitrary")),
        name=name,
    )(proj, proj, proj, proj, xp, cs["wa"], cs["wb"], cs["conv_w"], cs["dtb"], cs["aneg"],
      cs["norm_w"])
    return on, sfin, tail


SBA_GROUP = 8


def _sba_prompt_kernel(pad, bias_ref, q_ref, k_ref, v_ref, z_ref, y_ref, acc_ref, car_ref):
    hg = pl.program_id(1)
    qi = pl.program_id(2)
    t = q_ref.shape[0]
    hw = 2 * SBA_HD
    nheads = acc_ref.shape[0]
    first = lax.broadcasted_iota(jnp.int32, (1, hw), 1) < SBA_HD
    qh = []
    for p in range(nheads // 2):
        q = q_ref[:, p * hw:(p + 1) * hw] * (SBA_HD ** -0.5 * LOG2E)
        qh.append(jnp.where(first, q, 0.0).astype(BF16))
        qh.append(jnp.where(first, 0.0, q).astype(BF16))
    bias = [bias_ref[hg * nheads + h] * LOG2E for h in range(nheads)]
    r = lax.broadcasted_iota(jnp.int32, (2 * t, 2 * t), 0) % t
    c = lax.broadcasted_iota(jnp.int32, (2 * t, 2 * t), 1)
    later_ones = jnp.where(jnp.logical_or(c < r, c >= t), 1.0, 0.0).astype(BF16)
    acc_ref[...] = jnp.zeros(acc_ref.shape, F32)
    car_ref[...] = jnp.zeros(car_ref.shape, F32)

    def block(j, vis):
        start = pl.multiple_of(j * t, t)
        heads = range(nheads)
        kb = [k_ref[pl.ds(start, t), p * hw:(p + 1) * hw].astype(BF16) for p in range(nheads // 2)]
        vb = [v_ref[pl.ds(start, t), p * hw:(p + 1) * hw].astype(BF16) for p in range(nheads // 2)]
        z = [_dot_nt(qh[h], kb[h // 2]) + bias[h] for h in heads]
        sp = [jnp.maximum(z[h], 0.0) + jnp.log2(1.0 + jnp.exp2(-jnp.abs(z[h]))) for h in heads]
        cat = []
        for h in heads:
            lk = -sp[h] if vis is None else jnp.where(vis, -sp[h], 0.0)
            hi = lk.astype(BF16)
            cat.append(jnp.concatenate([hi, (lk - hi.astype(F32)).astype(BF16)], axis=1))
        res = [_dot(cat[h], later_ones) for h in heads]
        a = []
        for h in heads:
            ah = jnp.exp2(z[h] - sp[h] + (res[h][:, 0:t] + car_ref[h]))
            a.append((ah if vis is None else jnp.where(vis, ah, 0.0)).astype(BF16))
        o = [_dot(a[h], vb[h // 2]) for h in heads]
        for h in heads:
            acc_ref[h] += o[h]
            car_ref[h] += res[h][:, t:2 * t]

    qrow = lax.broadcasted_iota(jnp.int32, (t, t), 0)
    klane = lax.broadcasted_iota(jnp.int32, (t, t), 1)
    block(qi, jnp.logical_and(klane < qrow, qi * t + klane >= pad))

    def body(jj, carry):
        block(qi - 1 - jj, None)
        return carry

    lax.fori_loop(0, jnp.maximum(qi - 1, 0), body, 0)

    @pl.when(qi > 0)
    def _():
        block(0, klane >= pad)

    for p in range(nheads // 2):
        o = jnp.where(first, acc_ref[2 * p], acc_ref[2 * p + 1])
        y_ref[:, p * hw:(p + 1) * hw] = (o * _silu(z_ref[:, p * hw:(p + 1) * hw])).astype(y_ref.dtype)


def _sba_prompt(xp, in_w, sb_bias, pad, nb, lp, name):
    t = CHUNK
    nq = lp // t
    gw = SBA_GROUP * SBA_HD
    ngrp = SBA_H // SBA_GROUP
    proj = _matmul(xp, in_w, name + "_in")
    y = pl.pallas_call(
        functools.partial(_sba_prompt_kernel, pad),
        out_shape=jax.ShapeDtypeStruct((nb * lp, SBA_W), BF16),
        grid=(nb, ngrp, nq),
        in_specs=[pl.BlockSpec(memory_space=pltpu.SMEM),
                  pl.BlockSpec((t, gw), lambda b, p, i: (b * nq + i, p)),
                  pl.BlockSpec((lp, gw), lambda b, p, i: (b, ngrp + p)),
                  pl.BlockSpec((lp, gw), lambda b, p, i: (b, 2 * ngrp + p)),
                  pl.BlockSpec((t, gw), lambda b, p, i: (b * nq + i, 3 * ngrp + p))],
        out_specs=pl.BlockSpec((t, gw), lambda b, p, i: (b * nq + i, p)),
        scratch_shapes=[pltpu.VMEM((SBA_GROUP, t, 2 * SBA_HD), F32),
                        pltpu.VMEM((SBA_GROUP, t, t), F32)],
        compiler_params=_cparams(("parallel", "parallel", "arbitrary")),
        name=name,
    )(sb_bias.astype(F32), proj, proj, proj, proj)
    return y, proj


DEC_ROWS = 32
DEC_SEQS = 8


def _dec_conv(parts, st_ref, convw_ref, newst_ref):
    ncol = convw_ref.shape[1]
    raw = jnp.concatenate([p[...] for p in parts], axis=1)
    conv = convw_ref[CONV_W - 1:CONV_W, :] * raw
    for j in range(CONV_W - 1):
        conv = conv + convw_ref[j:j + 1, :] * st_ref[:, j * ncol:(j + 1) * ncol]
    newst_ref[:, 0:(CONV_W - 2) * ncol] = st_ref[:, ncol:(CONV_W - 1) * ncol]
    newst_ref[:, (CONV_W - 2) * ncol:] = raw
    return conv


def _ssd_dec_pre_kernel(xs_ref, bc_ref, x_ref, st_ref, wdt_ref, convw_ref, convb_ref, dtb_ref,
                        aneg_ref, expand_ref, newst_ref, xsa_ref, xdt_ref, bca_ref, dec_ref):
    conv = _dec_conv((xs_ref, bc_ref), st_ref, convw_ref, newst_ref)
    xbc = _silu(conv + convb_ref[...])
    dt = _softplus(_dot(x_ref[...].astype(BF16), wdt_ref[...]) + dtb_ref[...])
    dec_ref[...] = jnp.exp(dt * aneg_ref[...])
    xsa = xbc[:, :SSD_DI]
    xsa_ref[...] = xsa
    xdt_ref[...] = xsa * _dot_x01(dt, expand_ref[...])
    bca_ref[...] = xbc[:, SSD_DI:]


def _ssd_dec_state_kernel(dec_ref, h_ref, xdt_ref, bm_ref, cm_ref, xsa_ref, z_ref, dskip_ref,
                          normw_ref, hout_ref, yn_ref):
    blk = pl.program_id(0)
    nseq = h_ref.shape[0]
    gw = SSD_R * SSD_HD
    nrow = nseq * SSD_G
    gmask = (lax.broadcasted_iota(jnp.int32, (SSD_G, SSD_DI), 1) // gw
             == lax.broadcasted_iota(jnp.int32, (SSD_G, SSD_DI), 0))
    pieces = [jnp.where(gmask, jnp.broadcast_to(xdt_ref[i:i + 1, :], (SSD_G, SSD_DI)), 0.0)
              for i in range(nseq)]
    pieces.append(jnp.zeros((LANES - nrow, SSD_DI), F32))
    a_t = jnp.concatenate(pieces, axis=0).T.astype(BF16)
    bm = jnp.concatenate([bm_ref[...], jnp.zeros((LANES - nrow, SSD_N), F32)], axis=0)
    rid = lax.broadcasted_iota(jnp.int32, (LANES, 1), 0) // SSD_G
    y_rows = []
    for i in range(nseq):
        outer = _dot(a_t, jnp.where(rid == i, bm, 0.0).astype(BF16))
        c_i = cm_ref[i * SSD_G:(i + 1) * SSD_G, :].astype(BF16)
        y_parts = []
        for g in range(SSD_G):
            hn = []
            for r in range(SSD_R):
                h = g * SSD_R + r
                hs = slice(h * SSD_HD, (h + 1) * SSD_HD)
                d = dec_ref[(blk * nseq + i) * SSD_H + h]
                v = h_ref[i, hs, :] * d + outer[hs, :]
                hout_ref[i, hs, :] = v
                hn.append(v)
            hg = jnp.concatenate(hn, axis=0).astype(BF16)
            y_parts.append(_dot_nt(c_i, hg)[g:g + 1, :])
        y_rows.append(jnp.concatenate(y_parts, axis=1))
    y = jnp.concatenate(y_rows, axis=0) + dskip_ref[...] * xsa_ref[...]
    y = y * _silu(z_ref[...])
    ms = jnp.mean(y * y, axis=-1, keepdims=True)
    yn_ref[...] = y * lax.rsqrt(ms + EPS) * normw_ref[...]


def _ssd_decode(xs, state, conv_state, cs, name):
    db = xs.shape[0]
    rb = _pick(db, (DEC_ROWS, 8))
    proj = _matmul(xs, cs["main"], name + "_in")
    const = lambda i: (0, 0)
    nst = (CONV_W - 1) * SSD_CONV
    newst, xsa, xdt, bca, dec = pl.pallas_call(
        _ssd_dec_pre_kernel,
        out_shape=(jax.ShapeDtypeStruct((db, nst), F32),
                   jax.ShapeDtypeStruct((db, SSD_DI), F32),
                   jax.ShapeDtypeStruct((db, SSD_DI), F32),
                   jax.ShapeDtypeStruct((db, 2 * SSD_G * SSD_N), F32),
                   jax.ShapeDtypeStruct((db, LANES), F32)),
        grid=(db // rb,),
        in_specs=[pl.BlockSpec((rb, SSD_DI), lambda i: (i, 1)),
                  pl.BlockSpec((rb, SSD_DI), lambda i: (i, 2)),
                  pl.BlockSpec((rb, D_MODEL), lambda i: (i, 0)),
                  pl.BlockSpec((rb, nst), lambda i: (i, 0)),
                  pl.BlockSpec((D_MODEL, LANES), const),
                  pl.BlockSpec((CONV_W, SSD_CONV), const),
                  pl.BlockSpec((1, SSD_CONV), const),
                  pl.BlockSpec((1, LANES), const),
                  pl.BlockSpec((1, LANES), const),
                  pl.BlockSpec((LANES, SSD_DI), const)],
        out_specs=(pl.BlockSpec((rb, nst), lambda i: (i, 0)),
                   pl.BlockSpec((rb, SSD_DI), lambda i: (i, 0)),
                   pl.BlockSpec((rb, SSD_DI), lambda i: (i, 0)),
                   pl.BlockSpec((rb, 2 * SSD_G * SSD_N), lambda i: (i, 0)),
                   pl.BlockSpec((rb, LANES), lambda i: (i, 0))),
        compiler_params=_cparams(("parallel",)),
        name=name + "_pre",
    )(proj, proj, xs, conv_state.reshape(db, nst), cs["wdt"], cs["conv_w"], cs["conv_b"],
      cs["dtb"], cs["aneg"], cs["expand"])
    ns = DEC_SEQS
    bm = bca[:, :SSD_G * SSD_N].reshape(db * SSD_G, SSD_N)
    cm = bca[:, SSD_G * SSD_N:].reshape(db * SSD_G, SSD_N)
    hout, yn = pl.pallas_call(
        _ssd_dec_state_kernel,
        out_shape=(jax.ShapeDtypeStruct((db, SSD_DI, SSD_N), F32),
                   jax.ShapeDtypeStruct((db, SSD_DI), F32)),
        grid=(db // ns,),
        in_specs=[pl.BlockSpec(memory_space=pltpu.SMEM),
                  pl.BlockSpec((ns, SSD_DI, SSD_N), lambda i: (i, 0, 0)),
                  pl.BlockSpec((ns, SSD_DI), lambda i: (i, 0)),
                  pl.BlockSpec((ns * SSD_G, SSD_N), lambda i: (i, 0)),
                  pl.BlockSpec((ns * SSD_G, SSD_N), lambda i: (i, 0)),
                  pl.BlockSpec((ns, SSD_DI), lambda i: (i, 0)),
                  pl.BlockSpec((ns, SSD_DI), lambda i: (i, 0)),
                  pl.BlockSpec((1, SSD_DI), const),
                  pl.BlockSpec((1, SSD_DI), const)],
        out_specs=(pl.BlockSpec((ns, SSD_DI, SSD_N), lambda i: (i, 0, 0)),
                   pl.BlockSpec((ns, SSD_DI), lambda i: (i, 0))),
        compiler_params=_cparams(("parallel",)),
        name=name + "_state",
    )(dec[:, :SSD_H].reshape(db * SSD_H), state.reshape(db, SSD_DI, SSD_N), xdt, bm, cm, xsa,
      proj, cs["dskip_e"], cs["norm_w"])
    return (yn, hout.reshape(db, SSD_H, SSD_HD, SSD_N),
            newst.reshape(db, CONV_W - 1, SSD_CONV))


def _gdn_dec_pre_kernel(q_ref, k_ref, v_ref, x_ref, st_ref, wa_ref, wb_ref, convw_ref, dtb_ref,
                        aneg_ref, newst_ref, qn_ref, kn_ref, va_ref, eg_ref, beta_ref, qk_ref):
    conv = _dec_conv((q_ref, k_ref, v_ref), st_ref, convw_ref, newst_ref)
    qkv = _silu(conv)
    xb = x_ref[...].astype(BF16)
    eg_ref[...] = jnp.exp(aneg_ref[...] * _softplus(_dot(xb, wa_ref[...]) + dtb_ref[...]))
    beta_ref[...] = _sigmoid(_dot(xb, wb_ref[...]))
    lane = lax.broadcasted_iota(jnp.int32, (1, LANES), 1)
    qk = jnp.zeros(qk_ref.shape, F32)
    for h in range(GDN_H):
        hs = slice(h * GDN_DK, (h + 1) * GDN_DK)
        q = qkv[:, hs]
        k = qkv[:, GDN_QK + h * GDN_DK:GDN_QK + (h + 1) * GDN_DK]
        q = q * lax.rsqrt(jnp.sum(q * q, axis=-1, keepdims=True) + 1e-6) * (GDN_DK ** -0.5)
        k = k * lax.rsqrt(jnp.sum(k * k, axis=-1, keepdims=True) + 1e-6)
        qn_ref[:, hs] = q
        kn_ref[:, hs] = k
        qk = jnp.where(lane == h, jnp.sum(q * k, axis=-1, keepdims=True), qk)
    qk_ref[...] = qk
    va_ref[...] = qkv[:, 2 * GDN_QK:]


def _gdn_dec_state_kernel(eg_ref, beta_ref, qk_ref, s_ref, q_ref, k_ref, v_ref, z_ref, normw_ref,
                          sout_ref, on_ref):
    blk = pl.program_id(0)
    nseq = s_ref.shape[0]
    nrow = nseq * GDN_H
    kparts = [k_ref[:, h * GDN_DK:(h + 1) * GDN_DK] for h in range(GDN_H)]
    kparts.append(jnp.zeros((LANES - nrow, GDN_DK), F32))
    k_t = jnp.concatenate(kparts, axis=0).T.astype(BF16)
    rid = lax.broadcasted_iota(jnp.int32, (LANES, 1), 0)
    o_cols = []
    for h in range(GDN_H):
        hs = slice(h * GDN_DK, (h + 1) * GDN_DK)
        vs = slice(h * GDN_DV, (h + 1) * GDN_DV)
        kq = jnp.concatenate([k_ref[:, hs], q_ref[:, hs]], axis=0).astype(BF16)
        o_rows = []
        for i in range(nseq):
            sidx = (blk * nseq + i) * GDN_H + h
            eg = eg_ref[sidx]
            s = s_ref[i, h]
            r = _dot(kq, s.astype(BF16))
            v_new = beta_ref[sidx] * (v_ref[i:i + 1, vs] - eg * r[i:i + 1, :])
            o_rows.append(eg * r[nseq + i:nseq + i + 1, :] + qk_ref[sidx] * v_new)
            rm = jnp.where(rid == h * nseq + i, jnp.broadcast_to(v_new, (LANES, GDN_DV)), 0.0)
            sout_ref[i, h] = s * eg + _dot(k_t, rm.astype(BF16))
        o = jnp.concatenate(o_rows, axis=0)
        ms = jnp.mean(o * o, axis=-1, keepdims=True)
        o_cols.append(o * lax.rsqrt(ms + EPS) * normw_ref[...] * _silu(z_ref[:, vs]))
    on_ref[...] = jnp.concatenate(o_cols, axis=1)


def _gdn_decode(xs, state, conv_state, cs, name):
    db = xs.shape[0]
    rb = _pick(db, (DEC_ROWS, 8))
    proj = _matmul(xs, cs["main"], name + "_in")
    const = lambda i: (0, 0)
    nst = (CONV_W - 1) * GDN_CONV
    row = lambda w: pl.BlockSpec((rb, w), lambda i: (i, 0))
    newst, qn, kn, va, eg, beta, qk = pl.pallas_call(
        _gdn_dec_pre_kernel,
        out_shape=(jax.ShapeDtypeStruct((db, nst), F32),
                   jax.ShapeDtypeStruct((db, GDN_QK), F32),
                   jax.ShapeDtypeStruct((db, GDN_QK), F32),
                   jax.ShapeDtypeStruct((db, GDN_VW), F32),
                   jax.ShapeDtypeStruct((db, LANES), F32),
                   jax.ShapeDtypeStruct((db, LANES), F32),
                   jax.ShapeDtypeStruct((db, LANES), F32)),
        grid=(db // rb,),
        in_specs=[pl.BlockSpec((rb, GDN_QK), lambda i: (i, 0)),
                  pl.BlockSpec((rb, GDN_QK), lambda i: (i, 1)),
                  pl.BlockSpec((rb, GDN_VW), lambda i: (i, 1)),
                  row(D_MODEL), row(nst),
                  pl.BlockSpec((D_MODEL, LANES), const),
                  pl.BlockSpec((D_MODEL, LANES), const),
                  pl.BlockSpec((CONV_W, GDN_CONV), const),
                  pl.BlockSpec((1, LANES), const),
                  pl.BlockSpec((1, LANES), const)],
        out_specs=(row(nst), row(GDN_QK), row(GDN_QK), row(GDN_VW), row(LANES), row(LANES),
                   row(LANES)),
        compiler_params=_cparams(("parallel",)),
        name=name + "_pre",
    )(proj, proj, proj, xs, conv_state.reshape(db, nst), cs["wa"], cs["wb"], cs["conv_w"],
      cs["dtb"], cs["aneg"])
    ns = DEC_SEQS
    smem = pl.BlockSpec(memory_space=pltpu.SMEM)
    flat = lambda a: a[:, :GDN_H].reshape(db * GDN_H)
    sout, on = pl.pallas_call(
        _gdn_dec_state_kernel,
        out_shape=(jax.ShapeDtypeStruct((db, GDN_H, GDN_DK, GDN_DV), F32),
                   jax.ShapeDtypeStruct((db, GDN_VW), F32)),
        grid=(db // ns,),
        in_specs=[smem, smem, smem,
                  pl.BlockSpec((ns, GDN_H, GDN_DK, GDN_DV), lambda i: (i, 0, 0, 0)),
                  pl.BlockSpec((ns, GDN_QK), lambda i: (i, 0)),
                  pl.BlockSpec((ns, GDN_QK), lambda i: (i, 0)),
                  pl.BlockSpec((ns, GDN_VW), lambda i: (i, 0)),
                  pl.BlockSpec((ns, GDN_VW), lambda i: (i, 2)),
                  pl.BlockSpec((1, GDN_DV), const)],
        out_specs=(pl.BlockSpec((ns, GDN_H, GDN_DK, GDN_DV), lambda i: (i, 0, 0, 0)),
                   pl.BlockSpec((ns, GDN_VW), lambda i: (i, 0))),
        compiler_params=_cparams(("parallel",)),
        name=name + "_state",
    )(flat(eg), flat(beta), flat(qk), state, qn, kn, va, proj, cs["norm_w"])
    return on, sout, newst.reshape(db, CONV_W - 1, GDN_CONV)


DEC_PAGES = 4


def _sba_dec_kernel(pt_ref, q_ref, zg_ref, bias_ref, *rest):
    npg = DEC_PAGES
    k_refs = rest[:npg]
    v_refs = rest[npg:2 * npg]
    y_ref, acc_ref, car_ref = rest[2 * npg:]
    s = pl.program_id(1)
    nblk = k_refs[0].shape[1] // LANES

    @pl.when(s == 0)
    def _():
        acc_ref[...] = jnp.zeros(acc_ref.shape, F32)
        car_ref[...] = jnp.zeros(car_ref.shape, F32)

    qb = (q_ref[0] * (SBA_HD ** -0.5 * LOG2E)).astype(BF16)
    bias = bias_ref[...] * LOG2E
    own = (lax.broadcasted_iota(jnp.int32, (SBA_H, LANES), 1) % SBA_H
           == lax.broadcasted_iota(jnp.int32, (SBA_H, LANES), 0))
    r = lax.broadcasted_iota(jnp.int32, (2 * LANES, 2 * LANES), 0) % LANES
    c = lax.broadcasted_iota(jnp.int32, (2 * LANES, 2 * LANES), 1)
    later_ones = jnp.where(jnp.logical_or(r > c, c >= LANES), 1.0, 0.0).astype(BF16)

    pages = range(npg)
    blocks = range(nblk)
    zz = [_dot_nt(qb, k_refs[p][0].astype(BF16)) for p in pages]
    z = [[zz[p][:, b * LANES:(b + 1) * LANES] + bias for b in blocks] for p in pages]
    sp = [[jnp.maximum(z[p][b], 0.0) + jnp.log2(1.0 + jnp.exp2(-jnp.abs(z[p][b]))) for b in blocks]
          for p in pages]
    cat = []
    for p in pages:
        for b in blocks:
            lk = jnp.where(own, -sp[p][b], 0.0)
            hi = lk.astype(BF16)
            cat.append(jnp.concatenate([hi, (lk - hi.astype(F32)).astype(BF16)], axis=1))
    res_all = _dot(jnp.concatenate(cat, axis=0), later_ones)
    res = [[res_all[(p * nblk + b) * SBA_H:(p * nblk + b + 1) * SBA_H] for b in blocks] for p in pages]
    car = car_ref[...]
    o = acc_ref[...]
    for p in pages:
        a = [None] * nblk
        for b in reversed(blocks):
            e = jnp.exp2(z[p][b] - sp[p][b] + (res[p][b][:, 0:LANES] + car))
            a[b] = jnp.where(own, e, 0.0).astype(BF16)
            car = car + res[p][b][:, LANES:2 * LANES]
        o = o + _dot(jnp.concatenate(a, axis=1), v_refs[p][0].astype(BF16))
    car_ref[...] = car
    acc_ref[...] = o

    @pl.when(s == pl.num_programs(1) - 1)
    def _():
        y_ref[0] = o * _silu(zg_ref[0])


def _sba_decode(xs, cache_k, cache_v, page_table, in_w, sb_bias, name):
    db = xs.shape[0]
    n_pool, ps = cache_k.shape[0], cache_k.shape[1]
    npages = page_table.shape[1]
    npg = DEC_PAGES
    assert npages % npg == 0 and (ps * SBA_H) % LANES == 0 and LANES % SBA_H == 0
    proj = _matmul(xs, in_w, name + "_in")
    q = proj[:, :SBA_W].reshape(db, SBA_H, SBA_HD)
    zg = proj[:, 3 * SBA_W:].reshape(db, SBA_H, SBA_HD)
    bias = jnp.broadcast_to(sb_bias.astype(F32)[:, None], (SBA_H, LANES))
    kpool = cache_k.reshape(n_pool, ps * SBA_H, SBA_HD)
    vpool = cache_v.reshape(n_pool, ps * SBA_H, SBA_HD)

    def page_spec(r):
        return pl.BlockSpec((1, ps * SBA_H, SBA_HD),
                            lambda b, s, pt: (pt[b * npages + npages - 1 - (s * npg + r)], 0, 0))

    one = pl.BlockSpec((1, SBA_H, SBA_HD), lambda b, s, pt: (b, 0, 0))
    y = pl.pallas_call(
        _sba_dec_kernel,
        out_shape=jax.ShapeDtypeStruct((db, SBA_H, SBA_HD), F32),
        grid_spec=pltpu.PrefetchScalarGridSpec(
            num_scalar_prefetch=1,
            grid=(db, npages // npg),
            in_specs=[one, one, pl.BlockSpec((SBA_H, LANES), lambda b, s, pt: (0, 0))]
                     + [page_spec(r) for r in range(npg)] + [page_spec(r) for r in range(npg)],
            out_specs=one,
            scratch_shapes=[pltpu.VMEM((SBA_H, SBA_HD), F32),
                            pltpu.VMEM((SBA_H, LANES), F32)]),
        compiler_params=_cparams(("parallel", "arbitrary")),
        name=name,
    )(page_table.reshape(-1).astype(jnp.int32), q, zg, bias, *([kpool] * npg), *([vpool] * npg))
    return y.reshape(db, SBA_W), proj


def _pad_prompt(x_prompt, meta_tokens):
    nb, seq, d = x_prompt.shape
    n_meta = meta_tokens.shape[0]
    real = n_meta + seq
    pad = (-real) % CHUNK
    lp = pad + real
    meta = jnp.broadcast_to(meta_tokens[None].astype(x_prompt.dtype), (nb, n_meta, d))
    xp = jnp.concatenate([jnp.zeros((nb, pad, d), x_prompt.dtype), meta, x_prompt], axis=1)
    return xp.reshape(nb * lp, d), pad, lp


def kernel(x_prompt, x_sample, state_l0_ssm, state_l0_conv, state_l1_delta, state_l1_conv, cache_l2_k, cache_l2_v, state_l3_ssm, state_l3_conv, page_table, meta_tokens, l0_in_w, l0_conv_w, l0_conv_b, l0_dt_bias, l0_a_log, l0_d_skip, l0_norm_w, l0_out_w, l0_ln_g, l0_ln_b, l1_in_w, l1_conv_w, l1_dt_bias, l1_a_log, l1_norm_w, l1_out_w, l1_ln_g, l1_ln_b, l2_in_w, l2_sb_bias, l2_out_w, l2_ln_g, l2_ln_b, l3_in_w, l3_conv_w, l3_conv_b, l3_dt_bias, l3_a_log, l3_d_skip, l3_norm_w, l3_out_w, l3_ln_g, l3_ln_b):
    nb, seq, d = x_prompt.shape
    db = x_sample.shape[0]
    assert x_sample.shape[1] == 1 and d == D_MODEL
    xp, pad, lp = _pad_prompt(x_prompt, meta_tokens)
    real = lp - pad
    xs = x_sample.reshape(db, d)

    def ssd_layer(xp, xs, w, ln_g, ln_b, state, conv_state, name):
        cs = _ssd_consts(w)
        yn, p_ssm, p_conv = _ssd_prompt(xp, cs, pad, nb, lp, name + "_prompt")
        xp = _out_ln(yn, cs["out_w"], xp, ln_g, ln_b, name + "_prompt_out")
        yd, s_ssm, s_conv = _ssd_decode(xs, state, conv_state, cs, name + "_dec")
        xs = _out_ln(yd, cs["out_w"], xs, ln_g, ln_b, name + "_dec_out")
        return xp, xs, p_ssm, p_conv, s_ssm, s_conv.reshape(db, CONV_W - 1, SSD_CONV)

    xp, xs, p_l0_ssm, p_l0_conv, s_l0_ssm, s_l0_conv = ssd_layer(
        xp, xs, (l0_in_w, l0_conv_w, l0_conv_b, l0_dt_bias, l0_a_log, l0_d_skip, l0_norm_w, l0_out_w),
        l0_ln_g, l0_ln_b, state_l0_ssm, state_l0_conv, "l0")

    cs1 = _gdn_consts((l1_in_w, l1_conv_w, l1_dt_bias, l1_a_log, l1_norm_w, l1_out_w))
    on, p_l1_delta, p_l1_conv = _gdn_prompt(xp, cs1, pad, nb, lp, "l1_prompt")
    xp = _out_ln(on, cs1["out_w"], xp, l1_ln_g, l1_ln_b, "l1_prompt_out")
    od, s_l1_delta, s_l1_conv = _gdn_decode(xs, state_l1_delta, state_l1_conv, cs1, "l1_dec")
    xs = _out_ln(od, cs1["out_w"], xs, l1_ln_g, l1_ln_b, "l1_dec_out")

    w2_in = l2_in_w.astype(BF16)
    w2_out = l2_out_w.astype(BF16)
    ya, proj_p = _sba_prompt(xp, w2_in, l2_sb_bias, pad, nb, lp, "l2_prompt")
    xp = _out_ln(ya, w2_out, xp, l2_ln_g, l2_ln_b, "l2_prompt_out")
    proj_p = proj_p.reshape(nb, lp, 4 * SBA_W)[:, pad:]
    p_l2_k = proj_p[..., SBA_W:2 * SBA_W].reshape(nb, real, SBA_H, SBA_HD)
    p_l2_v = proj_p[..., 2 * SBA_W:3 * SBA_W].reshape(nb, real, SBA_H, SBA_HD)
    yd, proj_s = _sba_decode(xs, cache_l2_k, cache_l2_v, page_table, w2_in, l2_sb_bias, "l2_dec")
    xs = _out_ln(yd, w2_out, xs, l2_ln_g, l2_ln_b, "l2_dec_out")
    s_l2_k = proj_s[:, SBA_W:2 * SBA_W].reshape(db, 1, SBA_H, SBA_HD)
    s_l2_v = proj_s[:, 2 * SBA_W:3 * SBA_W].reshape(db, 1, SBA_H, SBA_HD)

    xp, xs, p_l3_ssm, p_l3_conv, s_l3_ssm, s_l3_conv = ssd_layer(
        xp, xs, (l3_in_w, l3_conv_w, l3_conv_b, l3_dt_bias, l3_a_log, l3_d_skip, l3_norm_w, l3_out_w),
        l3_ln_g, l3_ln_b, state_l3_ssm, state_l3_conv, "l3")

    y_prompt = xp.reshape(nb, lp, d)[:, pad + meta_tokens.shape[0]:]
    y_sample = xs.reshape(db, 1, d)
    return (y_prompt, y_sample,
            p_l0_ssm, p_l0_conv, p_l1_delta, p_l1_conv, p_l2_k, p_l2_v, p_l3_ssm, p_l3_conv,
            s_l0_ssm, s_l0_conv, s_l1_delta, s_l1_conv, s_l2_k, s_l2_v, s_l3_ssm, s_l3_conv)
```

```python
import functools
import math

import jax
import jax.numpy as jnp
from jax import lax
from jax.experimental import pallas as pl
from jax.experimental.pallas import tpu as pltpu

F32 = jnp.float32
BF16 = jnp.bfloat16

N_META = 16
CONV_W = 4
EPS = 1e-5
D_MODEL = 1024
SSD_DI = 2048
SSD_HD = 64
SSD_H = 32
SSD_G = 8
SSD_N = 128
SSD_R = SSD_H // SSD_G
SSD_CONV = SSD_DI + 2 * SSD_G * SSD_N
GDN_H = 8
GDN_DK = 128
GDN_DV = 256
GDN_QK = GDN_H * GDN_DK
GDN_VW = GDN_H * GDN_DV
GDN_CONV = 2 * GDN_QK + GDN_VW
SBA_H = 16
SBA_HD = 64
SBA_W = SBA_H * SBA_HD
DEPTH = 4
DN_ALPHA = (2 * DEPTH) ** 0.25

LANES = 128
LOG2E = 1.4426950408889634
CHUNK = 128
VMEM_LIMIT = 56 * 1024 * 1024


def _cparams(sem):
    return pltpu.CompilerParams(dimension_semantics=sem, vmem_limit_bytes=VMEM_LIMIT)


def _dot(a, b):
    return jnp.dot(a, b, preferred_element_type=F32)


def _dot_nt(a, b):
    return lax.dot_general(a, b, (((1,), (1,)), ((), ())), preferred_element_type=F32)


def _split(x, terms):
    parts = []
    r = x
    for i in range(terms):
        p = r.astype(BF16)
        parts.append(p)
        if i + 1 < terms:
            r = r - p.astype(F32)
    return parts


def _dot_x01(x, e, terms=3):
    m = x.shape[0]
    parts = _split(x, terms)
    r = _dot(jnp.concatenate(parts, axis=0), e)
    out = r[0:m]
    for i in range(1, terms):
        out = out + r[i * m:(i + 1) * m]
    return out


def _dot_01x(e, x, terms=3):
    n = x.shape[1]
    parts = _split(x, terms)
    r = _dot(e, jnp.concatenate(parts, axis=1))
    out = r[:, 0:n]
    for i in range(1, terms):
        out = out + r[:, i * n:(i + 1) * n]
    return out


def _sigmoid(x):
    return 1.0 / (1.0 + jnp.exp(-x))


def _silu(x):
    return x * _sigmoid(x)


def _softplus(x):
    return jnp.maximum(x, 0.0) + jnp.log1p(jnp.exp(-jnp.abs(x)))


def _tri(n, strict):
    r = lax.broadcasted_iota(jnp.int32, (n, n), 0)
    c = lax.broadcasted_iota(jnp.int32, (n, n), 1)
    return (c < r) if strict else (c <= r)


def _mm_kernel(x_ref, w_ref, o_ref, xb_ref):
    @pl.when(pl.program_id(1) == 0)
    def _():
        xb_ref[...] = x_ref[...].astype(BF16)

    o_ref[...] = _dot(xb_ref[...], w_ref[...]).astype(o_ref.dtype)


def _pick(n, cands):
    for c in cands:
        if n % c == 0:
            return c
    return n


def _matmul(x, w, name):
    m, k = x.shape
    n = w.shape[1]
    tm = _pick(m, (1536, 1024, 768, 512, 384, 256, 128))
    tn = _pick(n, (1024, 512, 256, 128))
    return pl.pallas_call(
        _mm_kernel,
        out_shape=jax.ShapeDtypeStruct((m, n), F32),
        grid=(m // tm, n // tn),
        in_specs=[pl.BlockSpec((tm, k), lambda i, j: (i, 0)),
                  pl.BlockSpec((k, tn), lambda i, j: (0, j))],
        out_specs=pl.BlockSpec((tm, tn), lambda i, j: (i, j)),
        scratch_shapes=[pltpu.VMEM((tm, k), BF16)],
        compiler_params=_cparams(("parallel", "arbitrary")),
        name=name,
    )(x, w)


def _out_ln_kernel(y_ref, w_ref, x_ref, g_ref, b_ref, o_ref):
    f = _dot(y_ref[...].astype(BF16), w_ref[...])
    r = DN_ALPHA * x_ref[...] + f
    mu = jnp.mean(r, axis=-1, keepdims=True)
    d = r - mu
    var = jnp.mean(d * d, axis=-1, keepdims=True)
    o_ref[...] = d * lax.rsqrt(var + EPS) * g_ref[...] + b_ref[...]


def _out_ln(y, w, x, g, b, name):
    m, k = y.shape
    d = w.shape[1]
    tm = _pick(m, (512, 384, 256, 128))
    return pl.pallas_call(
        _out_ln_kernel,
        out_shape=jax.ShapeDtypeStruct((m, d), F32),
        grid=(m // tm,),
        in_specs=[pl.BlockSpec((tm, k), lambda i: (i, 0)),
                  pl.BlockSpec((k, d), lambda i: (0, 0)),
                  pl.BlockSpec((tm, d), lambda i: (i, 0)),
                  pl.BlockSpec((1, d), lambda i: (0, 0)),
                  pl.BlockSpec((1, d), lambda i: (0, 0))],
        out_specs=pl.BlockSpec((tm, d), lambda i: (i, 0)),
        compiler_params=_cparams(("parallel",)),
        name=name,
    )(y, w, x, g.reshape(1, d), b.reshape(1, d))


def _conv_chunk(ext_ref, parts, valid, c, t, convw_ref, tail_ref, is_last):
    ncol = ext_ref.shape[1]

    @pl.when(c == 0)
    def _():
        ext_ref[0:8, :] = jnp.zeros((8, ncol), F32)

    @pl.when(c > 0)
    def _():
        ext_ref[0:8, :] = ext_ref[t:t + 8, :]

    off = 0
    for p in parts:
        w = p.shape[1]
        ext_ref[8:8 + t, off:off + w] = jnp.where(valid, p[...], 0.0)
        off += w

    @pl.when(is_last)
    def _():
        tail_ref[0] = ext_ref[t + 5:t + 8, :]

    acc = convw_ref[0:1, :] * ext_ref[5:5 + t, :]
    for j in range(1, CONV_W):
        acc = acc + convw_ref[j:j + 1, :] * ext_ref[5 + j:5 + j + t, :]
    return acc


def _ssd_prompt_kernel(pad, z_ref, xs_ref, bc_ref, x_ref, wdt_ref, convw_ref, convb_ref,
                       dtb_ref, aneg_ref, dskip_ref, normw_ref, expand_ref,
                       yn_ref, hfin_ref, tail_ref, ext_ref, ht_ref, y_ref):
    c = pl.program_id(1)
    is_last = c == pl.num_programs(1) - 1
    t = z_ref.shape[0]
    gw = SSD_R * SSD_HD

    @pl.when(c == 0)
    def _():
        ht_ref[...] = jnp.zeros(ht_ref.shape, F32)

    row = lax.broadcasted_iota(jnp.int32, (t, 1), 0)
    valid = jnp.logical_or(c > 0, row >= pad)

    conv = _conv_chunk(ext_ref, (xs_ref, bc_ref), valid, c, t, convw_ref, tail_ref, is_last)
    xbc = _silu(conv + convb_ref[...])
    xs = xbc[:, :SSD_DI]
    bm = xbc[:, SSD_DI:SSD_DI + SSD_G * SSD_N]
    cm = xbc[:, SSD_DI + SSD_G * SSD_N:]

    dt_raw = _dot(x_ref[...].astype(BF16), wdt_ref[...])
    dt = jnp.where(valid, _softplus(dt_raw + dtb_ref[...]), 0.0)
    a = dt * aneg_ref[...]
    ltri = jnp.where(_tri(t, False), 1.0, 0.0).astype(BF16)
    acum = _dot_01x(ltri, a)
    acum_t = acum.T

    both = _dot_x01(jnp.concatenate([dt, acum], axis=0), expand_ref[...])
    dt_e = both[0:t]
    acum_e = both[t:2 * t]
    last_e = acum_e[t - 1:t, :]
    eac_e = jnp.exp(acum_e)
    dte_e = jnp.exp(last_e - acum_e)
    cd_e = jnp.exp(last_e)

    xdt = xs * dt_e
    xw = xdt * dte_e
    tri = _tri(t, False)
    lane_g = lax.broadcasted_iota(jnp.int32, (1, gw), 1) // SSD_HD

    for g in range(SSD_G):
        gs = slice(g * gw, (g + 1) * gw)
        ns = slice(g * SSD_N, (g + 1) * SSD_N)
        bg = bm[:, ns]
        cg = cm[:, ns].astype(BF16)
        cb = _dot_nt(cg, bg.astype(BF16))
        xdt_g = xdt[:, gs].astype(BF16)
        ht = ht_ref[g]
        y_g = _dot(cg, ht.astype(BF16)) * eac_e[:, gs]
        for r in range(SSD_R):
            h = g * SSD_R + r
            diff = acum[:, h:h + 1] - acum_t[h:h + 1, :]
            seg = jnp.exp(jnp.where(tri, diff, -jnp.inf))
            yd = _dot((cb * seg).astype(BF16), xdt_g)
            y_g = y_g + jnp.where(lane_g == r, yd, 0.0)
        y_g = y_g + dskip_ref[:, gs] * xs[:, gs]
        y_ref[:, gs] = y_g
        ht_ref[g] = ht * cd_e[:, gs] + _dot(bg.T.astype(BF16), xw[:, gs].astype(BF16))

    y = y_ref[...] * _silu(z_ref[...])
    ms = jnp.mean(y * y, axis=-1, keepdims=True)
    yn_ref[...] = (y * lax.rsqrt(ms + EPS) * normw_ref[...]).astype(yn_ref.dtype)

    @pl.when(is_last)
    def _():
        for g in range(SSD_G):
            hfin_ref[0, g * gw:(g + 1) * gw, :] = ht_ref[g].T


def _ssd_consts(w):
    in_w, conv_w, conv_b, dt_bias, a_log, d_skip, norm_w, out_w = w
    main = in_w[:, :SSD_DI + SSD_CONV].astype(BF16)
    wdt = jnp.pad(in_w[:, SSD_DI + SSD_CONV:], ((0, 0), (0, LANES - SSD_H))).astype(BF16)
    dtb = jnp.pad(dt_bias.astype(F32), (0, LANES - SSD_H)).reshape(1, LANES)
    aneg = jnp.pad(-jnp.exp(a_log.astype(F32)), (0, LANES - SSD_H)).reshape(1, LANES)
    dskip_e = jnp.repeat(d_skip.astype(F32), SSD_HD).reshape(1, SSD_DI)
    hh = lax.broadcasted_iota(jnp.int32, (LANES, SSD_DI), 0)
    cc = lax.broadcasted_iota(jnp.int32, (LANES, SSD_DI), 1) // SSD_HD
    expand = (hh == cc).astype(BF16)
    return dict(main=main, wdt=wdt, dtb=dtb, aneg=aneg, dskip_e=dskip_e, expand=expand,
                conv_w=conv_w.astype(F32), conv_b=conv_b.astype(F32).reshape(1, SSD_CONV),
                norm_w=norm_w.astype(F32).reshape(1, SSD_DI), out_w=out_w.astype(BF16))


def _ssd_prompt(xp, cs, pad, nb, lp, name):
    t = CHUNK
    nc = lp // t
    proj = _matmul(xp, cs["main"], name + "_in")
    const = lambda b, c: (0, 0)
    rows = lambda b, c: (b * nc + c, 0)
    yn, hfin, tail = pl.pallas_call(
        functools.partial(_ssd_prompt_kernel, pad),
        out_shape=(jax.ShapeDtypeStruct((nb * lp, SSD_DI), BF16),
                   jax.ShapeDtypeStruct((nb, SSD_DI, SSD_N), F32),
                   jax.ShapeDtypeStruct((nb, CONV_W - 1, SSD_CONV), F32)),
        grid=(nb, nc),
        in_specs=[pl.BlockSpec((t, SSD_DI), lambda b, c: (b * nc + c, 0)),
                  pl.BlockSpec((t, SSD_DI), lambda b, c: (b * nc + c, 1)),
                  pl.BlockSpec((t, SSD_DI), lambda b, c: (b * nc + c, 2)),
                  pl.BlockSpec((t, D_MODEL), rows),
                  pl.BlockSpec((D_MODEL, LANES), const),
                  pl.BlockSpec((CONV_W, SSD_CONV), const),
                  pl.BlockSpec((1, SSD_CONV), const),
                  pl.BlockSpec((1, LANES), const),
                  pl.BlockSpec((1, LANES), const),
                  pl.BlockSpec((1, SSD_DI), const),
                  pl.BlockSpec((1, SSD_DI), const),
                  pl.BlockSpec((LANES, SSD_DI), const)],
        out_specs=(pl.BlockSpec((t, SSD_DI), rows),
                   pl.BlockSpec((1, SSD_DI, SSD_N), lambda b, c: (b, 0, 0)),
                   pl.BlockSpec((1, CONV_W - 1, SSD_CONV), lambda b, c: (b, 0, 0))),
        scratch_shapes=[pltpu.VMEM((t + 8, SSD_CONV), F32),
                        pltpu.VMEM((SSD_G, SSD_N, SSD_R * SSD_HD), F32),
                        pltpu.VMEM((t, SSD_DI), F32)],
        compiler_params=_cparams(("parallel", "arbitrary")),
        name=name,
    )(proj, proj, proj, xp, cs["wdt"], cs["conv_w"], cs["conv_b"], cs["dtb"], cs["aneg"],
      cs["dskip_e"], cs["norm_w"], cs["expand"])
    return yn, hfin.reshape(nb, SSD_H, SSD_HD, SSD_N), tail


INV_BLOCK = 16


def _unit_lower_inverse(mats, t):
    n = range(len(mats))
    r = lax.broadcasted_iota(jnp.int32, (t, t), 0)
    c = lax.broadcasted_iota(jnp.int32, (t, t), 1)
    b = INV_BLOCK
    diag = (r // b) == (c // b)
    eye = jnp.where(r == c, 1.0, 0.0)
    ap = [jnp.where(diag, a, 0.0) for a in mats]
    p = [eye - ap[i] for i in n]
    for _ in range(int(math.log2(b)) - 1):
        apb = [ap[i].astype(BF16) for i in n]
        ap = [_dot(apb[i], apb[i]) for i in n]
        prod = [_dot(p[i].astype(BF16), ap[i].astype(BF16)) for i in n]
        p = [p[i] + prod[i] for i in n]
    while b < t:
        sel = jnp.logical_and((r // (2 * b)) == (c // (2 * b)), (r // b) != (c // b))
        off = [jnp.where(sel, a, 0.0).astype(BF16) for a in mats]
        pb = [p[i].astype(BF16) for i in n]
        left = [_dot(pb[i], off[i]).astype(BF16) for i in n]
        corr = [_dot(left[i], pb[i]) for i in n]
        p = [p[i] - corr[i] for i in n]
        b *= 2
    return p


def _gdn_prompt_kernel(pad, q_ref, k_ref, v_ref, z_ref, x_ref, wa_ref, wb_ref, convw_ref,
                       dtb_ref, aneg_ref, normw_ref,
                       on_ref, sfin_ref, tail_ref, ext_ref, s_ref):
    c = pl.program_id(1)
    is_last = c == pl.num_programs(1) - 1
    t = q_ref.shape[0]

    @pl.when(c == 0)
    def _():
        s_ref[...] = jnp.zeros(s_ref.shape, F32)

    row = lax.broadcasted_iota(jnp.int32, (t, 1), 0)
    valid = jnp.logical_or(c > 0, row >= pad)

    conv = _conv_chunk(ext_ref, (q_ref, k_ref, v_ref), valid, c, t, convw_ref, tail_ref, is_last)
    qkv = _silu(conv)

    xb = x_ref[...].astype(BF16)
    a_raw = _dot(xb, wa_ref[...])
    b_raw = _dot(xb, wb_ref[...])
    gate = jnp.where(valid, aneg_ref[...] * _softplus(a_raw + dtb_ref[...]), 0.0)
    beta = jnp.where(valid, _sigmoid(b_raw), 0.0)
    ltri = jnp.where(_tri(t, False), 1.0, 0.0).astype(BF16)
    gc = _dot_01x(ltri, gate)
    gc_t = gc.T
    tri = _tri(t, False)
    strict = _tri(t, True)

    heads = range(GDN_H)
    q, k, v = [], [], []
    for h in heads:
        qh = qkv[:, h * GDN_DK:(h + 1) * GDN_DK]
        kh = qkv[:, GDN_QK + h * GDN_DK:GDN_QK + (h + 1) * GDN_DK]
        q.append(qh * lax.rsqrt(jnp.sum(qh * qh, axis=-1, keepdims=True) + 1e-6) * (GDN_DK ** -0.5))
        k.append(kh * lax.rsqrt(jnp.sum(kh * kh, axis=-1, keepdims=True) + 1e-6))
        v.append(qkv[:, 2 * GDN_QK + h * GDN_DV:2 * GDN_QK + (h + 1) * GDN_DV])
    gcol = [gc[:, h:h + 1] for h in heads]
    bcol = [beta[:, h:h + 1] for h in heads]
    dec = [jnp.exp(jnp.where(tri, gcol[h] - gc_t[h:h + 1, :], -jnp.inf)) for h in heads]
    eg = [jnp.exp(gcol[h]) for h in heads]
    kb = [k[h] * bcol[h] for h in heads]
    kbf = [k[h].astype(BF16) for h in heads]
    kk = [_dot_nt(kb[h].astype(BF16), kbf[h]) for h in heads]
    qk = [_dot_nt(q[h].astype(BF16), kbf[h]) for h in heads]
    a = [jnp.where(strict, kk[h] * dec[h], 0.0) for h in heads]
    qk = [(qk[h] * dec[h]).astype(BF16) for h in heads]
    tinv = [m.astype(BF16) for m in _unit_lower_inverse(a, t)]
    u = [_dot(tinv[h], (v[h] * bcol[h]).astype(BF16)) for h in heads]
    w = [_dot(tinv[h], (kb[h] * eg[h]).astype(BF16)) for h in heads]
    s = [s_ref[h] for h in heads]
    sb = [s[h].astype(BF16) for h in heads]
    ws = [_dot(w[h].astype(BF16), sb[h]) for h in heads]
    qs = [_dot((q[h] * eg[h]).astype(BF16), sb[h]) for h in heads]
    vnb = [(u[h] - ws[h]).astype(BF16) for h in heads]
    o = [qs[h] + _dot(qk[h], vnb[h]) for h in heads]
    gl = [gc[t - 1:t, h:h + 1] for h in heads]
    kd = [(k[h] * jnp.exp(gl[h] - gcol[h])).T.astype(BF16) for h in heads]
    upd = [_dot(kd[h], vnb[h]) for h in heads]
    for h in heads:
        s_ref[h] = s[h] * jnp.exp(gl[h]) + upd[h]
        ms = jnp.mean(o[h] * o[h], axis=-1, keepdims=True)
        zh = z_ref[:, h * GDN_DV:(h + 1) * GDN_DV]
        on_ref[:, h * GDN_DV:(h + 1) * GDN_DV] = (
            o[h] * lax.rsqrt(ms + EPS) * normw_ref[...] * _silu(zh)).astype(on_ref.dtype)

    @pl.when(is_last)
    def _():
        sfin_ref[0] = s_ref[...]


def _gdn_consts(w):
    in_w, conv_w, dt_bias, a_log, norm_w, out_w = w
    nmain = GDN_CONV + GDN_VW
    main = in_w[:, :nmain].astype(BF16)
    wa = jnp.pad(in_w[:, nmain:nmain + GDN_H], ((0, 0), (0, LANES - GDN_H))).astype(BF16)
    wb = jnp.pad(in_w[:, nmain + GDN_H:], ((0, 0), (0, LANES - GDN_H))).astype(BF16)
    dtb = jnp.pad(dt_bias.astype(F32), (0, LANES - GDN_H)).reshape(1, LANES)
    aneg = jnp.pad(-jnp.exp(a_log.astype(F32)), (0, LANES - GDN_H)).reshape(1, LANES)
    return dict(main=main, wa=wa, wb=wb, dtb=dtb, aneg=aneg, conv_w=conv_w.astype(F32),
                norm_w=norm_w.astype(F32).reshape(1, GDN_DV), out_w=out_w.astype(BF16))


def _gdn_prompt(xp, cs, pad, nb, lp, name):
    t = CHUNK
    nc = lp // t
    proj = _matmul(xp, cs["main"], name + "_in")
    const = lambda b, c: (0, 0)
    rows = lambda b, c: (b * nc + c, 0)
    on, sfin, tail = pl.pallas_call(
        functools.partial(_gdn_prompt_kernel, pad),
        out_shape=(jax.ShapeDtypeStruct((nb * lp, GDN_VW), BF16),
                   jax.ShapeDtypeStruct((nb, GDN_H, GDN_DK, GDN_DV), F32),
                   jax.ShapeDtypeStruct((nb, CONV_W - 1, GDN_CONV), F32)),
        grid=(nb, nc),
        in_specs=[pl.BlockSpec((t, GDN_QK), lambda b, c: (b * nc + c, 0)),
                  pl.BlockSpec((t, GDN_QK), lambda b, c: (b * nc + c, 1)),
                  pl.BlockSpec((t, GDN_VW), lambda b, c: (b * nc + c, 1)),
                  pl.BlockSpec((t, GDN_VW), lambda b, c: (b * nc + c, 2)),
                  pl.BlockSpec((t, D_MODEL), rows),
                  pl.BlockSpec((D_MODEL, LANES), const),
                  pl.BlockSpec((D_MODEL, LANES), const),
                  pl.BlockSpec((CONV_W, GDN_CONV), const),
                  pl.BlockSpec((1, LANES), const),
                  pl.BlockSpec((1, LANES), const),
                  pl.BlockSpec((1, GDN_DV), const)],
        out_specs=(pl.BlockSpec((t, GDN_VW), rows),
                   pl.BlockSpec((1, GDN_H, GDN_DK, GDN_DV), lambda b, c: (b, 0, 0, 0)),
                   pl.BlockSpec((1, CONV_W - 1, GDN_CONV), lambda b, c: (b, 0, 0))),
        scratch_shapes=[pltpu.VMEM((t + 8, GDN_CONV), F32),
                        pltpu.VMEM((GDN_H, GDN_DK, GDN_DV), F32)],
        compiler_params=_cparams(("parallel", "arbitrary")),
        name=name,
    )(proj, proj, proj, proj, xp, cs["wa"], cs["wb"], cs["conv_w"], cs["dtb"], cs["aneg"],
      cs["norm_w"])
    return on, sfin, tail


SBA_GROUP = 8


def _sba_prompt_kernel(pad, bias_ref, q_ref, k_ref, v_ref, z_ref, y_ref, acc_ref, car_ref):
    hg = pl.program_id(1)
    qi = pl.program_id(2)
    t = q_ref.shape[0]
    hw = 2 * SBA_HD
    nheads = acc_ref.shape[0]
    first = lax.broadcasted_iota(jnp.int32, (1, hw), 1) < SBA_HD
    qh = []
    for p in range(nheads // 2):
        q = q_ref[:, p * hw:(p + 1) * hw] * (SBA_HD ** -0.5 * LOG2E)
        qh.append(jnp.where(first, q, 0.0).astype(BF16))
        qh.append(jnp.where(first, 0.0, q).astype(BF16))
    bias = [bias_ref[hg * nheads + h] * LOG2E for h in range(nheads)]
    r = lax.broadcasted_iota(jnp.int32, (2 * t, 2 * t), 0) % t
    c = lax.broadcasted_iota(jnp.int32, (2 * t, 2 * t), 1)
    later_ones = jnp.where(jnp.logical_or(c < r, c >= t), 1.0, 0.0).astype(BF16)
    acc_ref[...] = jnp.zeros(acc_ref.shape, F32)
    car_ref[...] = jnp.zeros(car_ref.shape, F32)

    def block(js, vis):
        heads = range(nheads)
        nj = range(len(js))
        starts = [pl.multiple_of(j * t, t) for j in js]
        kb = [[k_ref[pl.ds(st, t), p * hw:(p + 1) * hw].astype(BF16) for p in range(nheads // 2)]
              for st in starts]
        vb = [[v_ref[pl.ds(st, t), p * hw:(p + 1) * hw].astype(BF16) for p in range(nheads // 2)]
              for st in starts]
        z = [[_dot_nt(qh[h], kb[i][h // 2]) + bias[h] for h in heads] for i in nj]
        sp = [[jnp.maximum(z[i][h], 0.0) + jnp.log2(1.0 + jnp.exp2(-jnp.abs(z[i][h]))) for h in heads]
              for i in nj]
        cat = []
        for i in nj:
            row = []
            for h in heads:
                lk = -sp[i][h] if vis is None else jnp.where(vis, -sp[i][h], 0.0)
                hi = lk.astype(BF16)
                row.append(jnp.concatenate([hi, (lk - hi.astype(F32)).astype(BF16)], axis=1))
            cat.append(row)
        res = [[_dot(cat[i][h], later_ones) for h in heads] for i in nj]
        car = [car_ref[h] for h in heads]
        a = []
        for i in nj:
            row = []
            for h in heads:
                ah = jnp.exp2(z[i][h] - sp[i][h] + (res[i][h][:, 0:t] + car[h]))
                row.append((ah if vis is None else jnp.where(vis, ah, 0.0)).astype(BF16))
                car[h] = car[h] + res[i][h][:, t:2 * t]
            a.append(row)
        o = [[_dot(a[i][h], vb[i][h // 2]) for h in heads] for i in nj]
        for h in heads:
            tot = o[0][h]
            for i in nj[1:]:
                tot = tot + o[i][h]
            acc_ref[h] += tot
            car_ref[h] = car[h]

    qrow = lax.broadcasted_iota(jnp.int32, (t, t), 0)
    klane = lax.broadcasted_iota(jnp.int32, (t, t), 1)
    block([qi], jnp.logical_and(klane < qrow, qi * t + klane >= pad))

    n_mid = jnp.maximum(qi - 1, 0)

    def body(jj, carry):
        j = qi - 1 - 2 * jj
        block([j, j - 1], None)
        return carry

    lax.fori_loop(0, n_mid // 2, body, 0)

    @pl.when(n_mid % 2 == 1)
    def _():
        block([1], None)

    @pl.when(qi > 0)
    def _():
        block([0], klane >= pad)

    for p in range(nheads // 2):
        o = jnp.where(first, acc_ref[2 * p], acc_ref[2 * p + 1])
        y_ref[:, p * hw:(p + 1) * hw] = (o * _silu(z_ref[:, p * hw:(p + 1) * hw])).astype(y_ref.dtype)


def _sba_prompt(xp, in_w, sb_bias, pad, nb, lp, name):
    t = CHUNK
    nq = lp // t
    gw = SBA_GROUP * SBA_HD
    ngrp = SBA_H // SBA_GROUP
    proj = _matmul(xp, in_w, name + "_in")
    y = pl.pallas_call(
        functools.partial(_sba_prompt_kernel, pad),
        out_shape=jax.ShapeDtypeStruct((nb * lp, SBA_W), BF16),
        grid=(nb, ngrp, nq),
        in_specs=[pl.BlockSpec(memory_space=pltpu.SMEM),
                  pl.BlockSpec((t, gw), lambda b, p, i: (b * nq + i, p)),
                  pl.BlockSpec((lp, gw), lambda b, p, i: (b, ngrp + p)),
                  pl.BlockSpec((lp, gw), lambda b, p, i: (b, 2 * ngrp + p)),
                  pl.BlockSpec((t, gw), lambda b, p, i: (b * nq + i, 3 * ngrp + p))],
        out_specs=pl.BlockSpec((t, gw), lambda b, p, i: (b * nq + i, p)),
        scratch_shapes=[pltpu.VMEM((SBA_GROUP, t, 2 * SBA_HD), F32),
                        pltpu.VMEM((SBA_GROUP, t, t), F32)],
        compiler_params=_cparams(("parallel", "parallel", "arbitrary")),
        name=name,
    )(sb_bias.astype(F32), proj, proj, proj, proj)
    return y, proj


DEC_ROWS = 32
DEC_SEQS = 8


def _dec_conv(parts, st_ref, convw_ref, newst_ref):
    ncol = convw_ref.shape[1]
    raw = jnp.concatenate([p[...] for p in parts], axis=1)
    conv = convw_ref[CONV_W - 1:CONV_W, :] * raw
    for j in range(CONV_W - 1):
        conv = conv + convw_ref[j:j + 1, :] * st_ref[:, j * ncol:(j + 1) * ncol]
    newst_ref[:, 0:(CONV_W - 2) * ncol] = st_ref[:, ncol:(CONV_W - 1) * ncol]
    newst_ref[:, (CONV_W - 2) * ncol:] = raw
    return conv


def _ssd_dec_pre_kernel(xs_ref, bc_ref, x_ref, st_ref, wdt_ref, convw_ref, convb_ref, dtb_ref,
                        aneg_ref, expand_ref, newst_ref, xsa_ref, xdt_ref, bca_ref, dec_ref):
    conv = _dec_conv((xs_ref, bc_ref), st_ref, convw_ref, newst_ref)
    xbc = _silu(conv + convb_ref[...])
    dt = _softplus(_dot(x_ref[...].astype(BF16), wdt_ref[...]) + dtb_ref[...])
    dec_ref[...] = jnp.exp(dt * aneg_ref[...])
    xsa = xbc[:, :SSD_DI]
    xsa_ref[...] = xsa
    xdt_ref[...] = xsa * _dot_x01(dt, expand_ref[...])
    bca_ref[...] = xbc[:, SSD_DI:]


def _ssd_dec_state_kernel(dec_ref, h_ref, xdt_ref, bm_ref, cm_ref, xsa_ref, z_ref, dskip_ref,
                          normw_ref, hout_ref, yn_ref):
    blk = pl.program_id(0)
    nseq = h_ref.shape[0]
    gw = SSD_R * SSD_HD
    nrow = nseq * SSD_G
    gmask = (lax.broadcasted_iota(jnp.int32, (SSD_G, SSD_DI), 1) // gw
             == lax.broadcasted_iota(jnp.int32, (SSD_G, SSD_DI), 0))
    pieces = [jnp.where(gmask, jnp.broadcast_to(xdt_ref[i:i + 1, :], (SSD_G, SSD_DI)), 0.0)
              for i in range(nseq)]
    pieces.append(jnp.zeros((LANES - nrow, SSD_DI), F32))
    a_t = jnp.concatenate(pieces, axis=0).T.astype(BF16)
    bm = jnp.concatenate([bm_ref[...], jnp.zeros((LANES - nrow, SSD_N), F32)], axis=0)
    rid = lax.broadcasted_iota(jnp.int32, (LANES, 1), 0) // SSD_G
    y_rows = []
    for i in range(nseq):
        outer = _dot(a_t, jnp.where(rid == i, bm, 0.0).astype(BF16))
        c_i = cm_ref[i * SSD_G:(i + 1) * SSD_G, :].astype(BF16)
        y_parts = []
        for g in range(SSD_G):
            hn = []
            for r in range(SSD_R):
                h = g * SSD_R + r
                hs = slice(h * SSD_HD, (h + 1) * SSD_HD)
                d = dec_ref[(blk * nseq + i) * SSD_H + h]
                v = h_ref[i, hs, :] * d + outer[hs, :]
                hout_ref[i, hs, :] = v
                hn.append(v)
            hg = jnp.concatenate(hn, axis=0).astype(BF16)
            y_parts.append(_dot_nt(c_i, hg)[g:g + 1, :])
        y_rows.append(jnp.concatenate(y_parts, axis=1))
    y = jnp.concatenate(y_rows, axis=0) + dskip_ref[...] * xsa_ref[...]
    y = y * _silu(z_ref[...])
    ms = jnp.mean(y * y, axis=-1, keepdims=True)
    yn_ref[...] = y * lax.rsqrt(ms + EPS) * normw_ref[...]


def _ssd_decode(xs, state, conv_state, cs, name):
    db = xs.shape[0]
    rb = _pick(db, (DEC_ROWS, 8))
    proj = _matmul(xs, cs["main"], name + "_in")
    const = lambda i: (0, 0)
    nst = (CONV_W - 1) * SSD_CONV
    newst, xsa, xdt, bca, dec = pl.pallas_call(
        _ssd_dec_pre_kernel,
        out_shape=(jax.ShapeDtypeStruct((db, nst), F32),
                   jax.ShapeDtypeStruct((db, SSD_DI), F32),
                   jax.ShapeDtypeStruct((db, SSD_DI), F32),
                   jax.ShapeDtypeStruct((db, 2 * SSD_G * SSD_N), F32),
                   jax.ShapeDtypeStruct((db, LANES), F32)),
        grid=(db // rb,),
        in_specs=[pl.BlockSpec((rb, SSD_DI), lambda i: (i, 1)),
                  pl.BlockSpec((rb, SSD_DI), lambda i: (i, 2)),
                  pl.BlockSpec((rb, D_MODEL), lambda i: (i, 0)),
                  pl.BlockSpec((rb, nst), lambda i: (i, 0)),
                  pl.BlockSpec((D_MODEL, LANES), const),
                  pl.BlockSpec((CONV_W, SSD_CONV), const),
                  pl.BlockSpec((1, SSD_CONV), const),
                  pl.BlockSpec((1, LANES), const),
                  pl.BlockSpec((1, LANES), const),
                  pl.BlockSpec((LANES, SSD_DI), const)],
        out_specs=(pl.BlockSpec((rb, nst), lambda i: (i, 0)),
                   pl.BlockSpec((rb, SSD_DI), lambda i: (i, 0)),
                   pl.BlockSpec((rb, SSD_DI), lambda i: (i, 0)),
                   pl.BlockSpec((rb, 2 * SSD_G * SSD_N), lambda i: (i, 0)),
                   pl.BlockSpec((rb, LANES), lambda i: (i, 0))),
        compiler_params=_cparams(("parallel",)),
        name=name + "_pre",
    )(proj, proj, xs, conv_state.reshape(db, nst), cs["wdt"], cs["conv_w"], cs["conv_b"],
      cs["dtb"], cs["aneg"], cs["expand"])
    ns = DEC_SEQS
    bm = bca[:, :SSD_G * SSD_N].reshape(db * SSD_G, SSD_N)
    cm = bca[:, SSD_G * SSD_N:].reshape(db * SSD_G, SSD_N)
    hout, yn = pl.pallas_call(
        _ssd_dec_state_kernel,
        out_shape=(jax.ShapeDtypeStruct((db, SSD_DI, SSD_N), F32),
                   jax.ShapeDtypeStruct((db, SSD_DI), F32)),
        grid=(db // ns,),
        in_specs=[pl.BlockSpec(memory_space=pltpu.SMEM),
                  pl.BlockSpec((ns, SSD_DI, SSD_N), lambda i: (i, 0, 0)),
                  pl.BlockSpec((ns, SSD_DI), lambda i: (i, 0)),
                  pl.BlockSpec((ns * SSD_G, SSD_N), lambda i: (i, 0)),
                  pl.BlockSpec((ns * SSD_G, SSD_N), lambda i: (i, 0)),
                  pl.BlockSpec((ns, SSD_DI), lambda i: (i, 0)),
                  pl.BlockSpec((ns, SSD_DI), lambda i: (i, 0)),
                  pl.BlockSpec((1, SSD_DI), const),
                  pl.BlockSpec((1, SSD_DI), const)],
        out_specs=(pl.BlockSpec((ns, SSD_DI, SSD_N), lambda i: (i, 0, 0)),
                   pl.BlockSpec((ns, SSD_DI), lambda i: (i, 0))),
        compiler_params=_cparams(("parallel",)),
        name=name + "_state",
    )(dec[:, :SSD_H].reshape(db * SSD_H), state.reshape(db, SSD_DI, SSD_N), xdt, bm, cm, xsa,
      proj, cs["dskip_e"], cs["norm_w"])
    return (yn, hout.reshape(db, SSD_H, SSD_HD, SSD_N),
            newst.reshape(db, CONV_W - 1, SSD_CONV))


def _gdn_dec_pre_kernel(q_ref, k_ref, v_ref, x_ref, st_ref, wa_ref, wb_ref, convw_ref, dtb_ref,
                        aneg_ref, newst_ref, qn_ref, kn_ref, va_ref, eg_ref, beta_ref, qk_ref):
    conv = _dec_conv((q_ref, k_ref, v_ref), st_ref, convw_ref, newst_ref)
    qkv = _silu(conv)
    xb = x_ref[...].astype(BF16)
    eg_ref[...] = jnp.exp(aneg_ref[...] * _softplus(_dot(xb, wa_ref[...]) + dtb_ref[...]))
    beta_ref[...] = _sigmoid(_dot(xb, wb_ref[...]))
    lane = lax.broadcasted_iota(jnp.int32, (1, LANES), 1)
    qk = jnp.zeros(qk_ref.shape, F32)
    for h in range(GDN_H):
        hs = slice(h * GDN_DK, (h + 1) * GDN_DK)
        q = qkv[:, hs]
        k = qkv[:, GDN_QK + h * GDN_DK:GDN_QK + (h + 1) * GDN_DK]
        q = q * lax.rsqrt(jnp.sum(q * q, axis=-1, keepdims=True) + 1e-6) * (GDN_DK ** -0.5)
        k = k * lax.rsqrt(jnp.sum(k * k, axis=-1, keepdims=True) + 1e-6)
        qn_ref[:, hs] = q
        kn_ref[:, hs] = k
        qk = jnp.where(lane == h, jnp.sum(q * k, axis=-1, keepdims=True), qk)
    qk_ref[...] = qk
    va_ref[...] = qkv[:, 2 * GDN_QK:]


def _gdn_dec_state_kernel(eg_ref, beta_ref, qk_ref, s_ref, q_ref, k_ref, v_ref, z_ref, normw_ref,
                          sout_ref, on_ref):
    blk = pl.program_id(0)
    nseq = s_ref.shape[0]
    nrow = nseq * GDN_H
    kparts = [k_ref[:, h * GDN_DK:(h + 1) * GDN_DK] for h in range(GDN_H)]
    kparts.append(jnp.zeros((LANES - nrow, GDN_DK), F32))
    k_t = jnp.concatenate(kparts, axis=0).T.astype(BF16)
    rid = lax.broadcasted_iota(jnp.int32, (LANES, 1), 0)
    o_cols = []
    for h in range(GDN_H):
        hs = slice(h * GDN_DK, (h + 1) * GDN_DK)
        vs = slice(h * GDN_DV, (h + 1) * GDN_DV)
        kq = jnp.concatenate([k_ref[:, hs], q_ref[:, hs]], axis=0).astype(BF16)
        o_rows = []
        for i in range(nseq):
            sidx = (blk * nseq + i) * GDN_H + h
            eg = eg_ref[sidx]
            s = s_ref[i, h]
            r = _dot(kq, s.astype(BF16))
            v_new = beta_ref[sidx] * (v_ref[i:i + 1, vs] - eg * r[i:i + 1, :])
            o_rows.append(eg * r[nseq + i:nseq + i + 1, :] + qk_ref[sidx] * v_new)
            rm = jnp.where(rid == h * nseq + i, jnp.broadcast_to(v_new, (LANES, GDN_DV)), 0.0)
            sout_ref[i, h] = s * eg + _dot(k_t, rm.astype(BF16))
        o = jnp.concatenate(o_rows, axis=0)
        ms = jnp.mean(o * o, axis=-1, keepdims=True)
        o_cols.append(o * lax.rsqrt(ms + EPS) * normw_ref[...] * _silu(z_ref[:, vs]))
    on_ref[...] = jnp.concatenate(o_cols, axis=1)


def _gdn_decode(xs, state, conv_state, cs, name):
    db = xs.shape[0]
    rb = _pick(db, (DEC_ROWS, 8))
    proj = _matmul(xs, cs["main"], name + "_in")
    const = lambda i: (0, 0)
    nst = (CONV_W - 1) * GDN_CONV
    row = lambda w: pl.BlockSpec((rb, w), lambda i: (i, 0))
    newst, qn, kn, va, eg, beta, qk = pl.pallas_call(
        _gdn_dec_pre_kernel,
        out_shape=(jax.ShapeDtypeStruct((db, nst), F32),
                   jax.ShapeDtypeStruct((db, GDN_QK), F32),
                   jax.ShapeDtypeStruct((db, GDN_QK), F32),
                   jax.ShapeDtypeStruct((db, GDN_VW), F32),
                   jax.ShapeDtypeStruct((db, LANES), F32),
                   jax.ShapeDtypeStruct((db, LANES), F32),
                   jax.ShapeDtypeStruct((db, LANES), F32)),
        grid=(db // rb,),
        in_specs=[pl.BlockSpec((rb, GDN_QK), lambda i: (i, 0)),
                  pl.BlockSpec((rb, GDN_QK), lambda i: (i, 1)),
                  pl.BlockSpec((rb, GDN_VW), lambda i: (i, 1)),
                  row(D_MODEL), row(nst),
                  pl.BlockSpec((D_MODEL, LANES), const),
                  pl.BlockSpec((D_MODEL, LANES), const),
                  pl.BlockSpec((CONV_W, GDN_CONV), const),
                  pl.BlockSpec((1, LANES), const),
                  pl.BlockSpec((1, LANES), const)],
        out_specs=(row(nst), row(GDN_QK), row(GDN_QK), row(GDN_VW), row(LANES), row(LANES),
                   row(LANES)),
        compiler_params=_cparams(("parallel",)),
        name=name + "_pre",
    )(proj, proj, proj, xs, conv_state.reshape(db, nst), cs["wa"], cs["wb"], cs["conv_w"],
      cs["dtb"], cs["aneg"])
    ns = DEC_SEQS
    smem = pl.BlockSpec(memory_space=pltpu.SMEM)
    flat = lambda a: a[:, :GDN_H].reshape(db * GDN_H)
    sout, on = pl.pallas_call(
        _gdn_dec_state_kernel,
        out_shape=(jax.ShapeDtypeStruct((db, GDN_H, GDN_DK, GDN_DV), F32),
                   jax.ShapeDtypeStruct((db, GDN_VW), F32)),
        grid=(db // ns,),
        in_specs=[smem, smem, smem,
                  pl.BlockSpec((ns, GDN_H, GDN_DK, GDN_DV), lambda i: (i, 0, 0, 0)),
                  pl.BlockSpec((ns, GDN_QK), lambda i: (i, 0)),
                  pl.BlockSpec((ns, GDN_QK), lambda i: (i, 0)),
                  pl.BlockSpec((ns, GDN_VW), lambda i: (i, 0)),
                  pl.BlockSpec((ns, GDN_VW), lambda i: (i, 2)),
                  pl.BlockSpec((1, GDN_DV), const)],
        out_specs=(pl.BlockSpec((ns, GDN_H, GDN_DK, GDN_DV), lambda i: (i, 0, 0, 0)),
                   pl.BlockSpec((ns, GDN_VW), lambda i: (i, 0))),
        compiler_params=_cparams(("parallel",)),
        name=name + "_state",
    )(flat(eg), flat(beta), flat(qk), state, qn, kn, va, proj, cs["norm_w"])
    return on, sout, newst.reshape(db, CONV_W - 1, GDN_CONV)


DEC_PAGES = 8


def _sba_dec_kernel(pt_ref, q_ref, zg_ref, bias_ref, *rest):
    npg = DEC_PAGES
    k_refs = rest[:npg]
    v_refs = rest[npg:2 * npg]
    y_ref, qb_ref, acc_ref, car_ref = rest[2 * npg:]
    s = pl.program_id(1)
    ps = k_refs[0].shape[3]
    rid = lax.broadcasted_iota(jnp.int32, (SBA_H, 1), 0)

    @pl.when(s == 0)
    def _():
        acc_ref[...] = jnp.zeros(acc_ref.shape, F32)
        car_ref[...] = jnp.zeros(car_ref.shape, F32)
        q = q_ref[0] * (SBA_HD ** -0.5 * LOG2E)
        q = jnp.concatenate([q, jnp.zeros((SBA_H, LANES - SBA_HD), F32)], axis=1)
        qt = jnp.concatenate([q, jnp.zeros((LANES - SBA_H, LANES), F32)], axis=0).T
        for h in range(SBA_H):
            qb_ref[h] = jnp.broadcast_to(qt[0:SBA_HD, h:h + 1], (SBA_HD, ps))

    bias = bias_ref[...] * LOG2E
    r = lax.broadcasted_iota(jnp.int32, (2 * ps, 2 * ps), 0) % ps
    c = lax.broadcasted_iota(jnp.int32, (2 * ps, 2 * ps), 1)
    later_ones = jnp.where(jnp.logical_or(r > c, c >= ps), 1.0, 0.0).astype(BF16)

    pages = range(npg)
    z, sp, cat = [], [], []
    for p in pages:
        zp = bias
        for h in range(SBA_H):
            zh = jnp.sum(k_refs[p][0, h] * qb_ref[h], axis=0, keepdims=True)
            zp = zp + jnp.where(rid == h, zh, 0.0)
        spp = jnp.maximum(zp, 0.0) + jnp.log2(1.0 + jnp.exp2(-jnp.abs(zp)))
        hi = (-spp).astype(BF16)
        cat.append(jnp.concatenate([hi, (-spp - hi.astype(F32)).astype(BF16)], axis=1))
        z.append(zp)
        sp.append(spp)
    res = _dot(jnp.concatenate(cat, axis=0), later_ones)
    car = car_ref[...]
    a = []
    for p in pages:
        rp = res[p * SBA_H:(p + 1) * SBA_H]
        a.append(jnp.exp2(z[p] - sp[p] + (rp[:, 0:ps] + car)))
        car = car + rp[:, ps:2 * ps]
    car_ref[...] = car
    for h in range(SBA_H):
        t = acc_ref[h]
        for p in pages:
            t = t + v_refs[p][0, h] * a[p][h:h + 1, :]
        acc_ref[h] = t

    @pl.when(s == pl.num_programs(1) - 1)
    def _():
        ones = jnp.ones((8, ps), BF16)
        o = jnp.zeros((SBA_H, SBA_HD), F32)
        for h in range(SBA_H):
            parts = _split(acc_ref[h], 3)
            oh = _dot_nt(ones, parts[0]) + _dot_nt(ones, parts[1]) + _dot_nt(ones, parts[2])
            o = o + jnp.where(rid == h, oh[0:1, :], 0.0)
        y_ref[0] = o * _silu(zg_ref[0])


def _sba_decode(xs, cache_k, cache_v, page_table, in_w, sb_bias, name):
    db = xs.shape[0]
    ps = cache_k.shape[1]
    npages = page_table.shape[1]
    npg = DEC_PAGES
    assert npages % npg == 0 and ps == LANES
    proj = _matmul(xs, in_w, name + "_in")
    q = proj[:, :SBA_W].reshape(db, SBA_H, SBA_HD)
    zg = proj[:, 3 * SBA_W:].reshape(db, SBA_H, SBA_HD)
    bias = jnp.broadcast_to(sb_bias.astype(F32)[:, None], (SBA_H, LANES))
    kpool = jnp.transpose(cache_k, (0, 2, 3, 1))
    vpool = jnp.transpose(cache_v, (0, 2, 3, 1))

    def page_spec(r):
        return pl.BlockSpec((1, SBA_H, SBA_HD, ps),
                            lambda b, s, pt: (pt[b * npages + npages - 1 - (s * npg + r)], 0, 0, 0))

    one = pl.BlockSpec((1, SBA_H, SBA_HD), lambda b, s, pt: (b, 0, 0))
    y = pl.pallas_call(
        _sba_dec_kernel,
        out_shape=jax.ShapeDtypeStruct((db, SBA_H, SBA_HD), F32),
        grid_spec=pltpu.PrefetchScalarGridSpec(
            num_scalar_prefetch=1,
            grid=(db, npages // npg),
            in_specs=[one, one, pl.BlockSpec((SBA_H, LANES), lambda b, s, pt: (0, 0))]
                     + [page_spec(r) for r in range(npg)] + [page_spec(r) for r in range(npg)],
            out_specs=one,
            scratch_shapes=[pltpu.VMEM((SBA_H, SBA_HD, ps), F32),
                            pltpu.VMEM((SBA_H, SBA_HD, ps), F32),
                            pltpu.VMEM((SBA_H, LANES), F32)]),
        compiler_params=_cparams(("parallel", "arbitrary")),
        name=name,
    )(page_table.reshape(-1).astype(jnp.int32), q, zg, bias, *([kpool] * npg), *([vpool] * npg))
    return y.reshape(db, SBA_W), proj


def _pad_prompt(x_prompt, meta_tokens):
    nb, seq, d = x_prompt.shape
    n_meta = meta_tokens.shape[0]
    real = n_meta + seq
    pad = (-real) % CHUNK
    lp = pad + real
    meta = jnp.broadcast_to(meta_tokens[None].astype(x_prompt.dtype), (nb, n_meta, d))
    xp = jnp.concatenate([jnp.zeros((nb, pad, d), x_prompt.dtype), meta, x_prompt], axis=1)
    return xp.reshape(nb * lp, d), pad, lp


def kernel(x_prompt, x_sample, state_l0_ssm, state_l0_conv, state_l1_delta, state_l1_conv, cache_l2_k, cache_l2_v, state_l3_ssm, state_l3_conv, page_table, meta_tokens, l0_in_w, l0_conv_w, l0_conv_b, l0_dt_bias, l0_a_log, l0_d_skip, l0_norm_w, l0_out_w, l0_ln_g, l0_ln_b, l1_in_w, l1_conv_w, l1_dt_bias, l1_a_log, l1_norm_w, l1_out_w, l1_ln_g, l1_ln_b, l2_in_w, l2_sb_bias, l2_out_w, l2_ln_g, l2_ln_b, l3_in_w, l3_conv_w, l3_conv_b, l3_dt_bias, l3_a_log, l3_d_skip, l3_norm_w, l3_out_w, l3_ln_g, l3_ln_b):
    nb, seq, d = x_prompt.shape
    db = x_sample.shape[0]
    assert x_sample.shape[1] == 1 and d == D_MODEL
    xp, pad, lp = _pad_prompt(x_prompt, meta_tokens)
    real = lp - pad
    xs = x_sample.reshape(db, d)

    def ssd_layer(xp, xs, w, ln_g, ln_b, state, conv_state, name):
        cs = _ssd_consts(w)
        yn, p_ssm, p_conv = _ssd_prompt(xp, cs, pad, nb, lp, name + "_prompt")
        xp = _out_ln(yn, cs["out_w"], xp, ln_g, ln_b, name + "_prompt_out")
        yd, s_ssm, s_conv = _ssd_decode(xs, state, conv_state, cs, name + "_dec")
        xs = _out_ln(yd, cs["out_w"], xs, ln_g, ln_b, name + "_dec_out")
        return xp, xs, p_ssm, p_conv, s_ssm, s_conv.reshape(db, CONV_W - 1, SSD_CONV)

    xp, xs, p_l0_ssm, p_l0_conv, s_l0_ssm, s_l0_conv = ssd_layer(
        xp, xs, (l0_in_w, l0_conv_w, l0_conv_b, l0_dt_bias, l0_a_log, l0_d_skip, l0_norm_w, l0_out_w),
        l0_ln_g, l0_ln_b, state_l0_ssm, state_l0_conv, "l0")

    cs1 = _gdn_consts((l1_in_w, l1_conv_w, l1_dt_bias, l1_a_log, l1_norm_w, l1_out_w))
    on, p_l1_delta, p_l1_conv = _gdn_prompt(xp, cs1, pad, nb, lp, "l1_prompt")
    xp = _out_ln(on, cs1["out_w"], xp, l1_ln_g, l1_ln_b, "l1_prompt_out")
    od, s_l1_delta, s_l1_conv = _gdn_decode(xs, state_l1_delta, state_l1_conv, cs1, "l1_dec")
    xs = _out_ln(od, cs1["out_w"], xs, l1_ln_g, l1_ln_b, "l1_dec_out")

    w2_in = l2_in_w.astype(BF16)
    w2_out = l2_out_w.astype(BF16)
    ya, proj_p = _sba_prompt(xp, w2_in, l2_sb_bias, pad, nb, lp, "l2_prompt")
    xp = _out_ln(ya, w2_out, xp, l2_ln_g, l2_ln_b, "l2_prompt_out")
    proj_p = proj_p.reshape(nb, lp, 4 * SBA_W)[:, pad:]
    p_l2_k = proj_p[..., SBA_W:2 * SBA_W].reshape(nb, real, SBA_H, SBA_HD)
    p_l2_v = proj_p[..., 2 * SBA_W:3 * SBA_W].reshape(nb, real, SBA_H, SBA_HD)
    yd, proj_s = _sba_decode(xs, cache_l2_k, cache_l2_v, page_table, w2_in, l2_sb_bias, "l2_dec")
    xs = _out_ln(yd, w2_out, xs, l2_ln_g, l2_ln_b, "l2_dec_out")
    s_l2_k = proj_s[:, SBA_W:2 * SBA_W].reshape(db, 1, SBA_H, SBA_HD)
    s_l2_v = proj_s[:, 2 * SBA_W:3 * SBA_W].reshape(db, 1, SBA_H, SBA_HD)

    xp, xs, p_l3_ssm, p_l3_conv, s_l3_ssm, s_l3_conv = ssd_layer(
        xp, xs, (l3_in_w, l3_conv_w, l3_conv_b, l3_dt_bias, l3_a_log, l3_d_skip, l3_norm_w, l3_out_w),
        l3_ln_g, l3_ln_b, state_l3_ssm, state_l3_conv, "l3")

    y_prompt = xp.reshape(nb, lp, d)[:, pad + meta_tokens.shape[0]:]
    y_sample = xs.reshape(db, 1, d)
    return (y_prompt, y_sample,
            p_l0_ssm, p_l0_conv, p_l1_delta, p_l1_conv, p_l2_k, p_l2_v, p_l3_ssm, p_l3_conv,
            s_l0_ssm, s_l0_conv, s_l1_delta, s_l1_conv, s_l2_k, s_l2_v, s_l3_ssm, s_l3_conv)
```

```python
import functools
import math

import jax
import jax.numpy as jnp
from jax import lax
from jax.experimental import pallas as pl
from jax.experimental.pallas import tpu as pltpu

F32 = jnp.float32
BF16 = jnp.bfloat16

N_META = 16
CONV_W = 4
EPS = 1e-5
D_MODEL = 1024
SSD_DI = 2048
SSD_HD = 64
SSD_H = 32
SSD_G = 8
SSD_N = 128
SSD_R = SSD_H // SSD_G
SSD_CONV = SSD_DI + 2 * SSD_G * SSD_N
GDN_H = 8
GDN_DK = 128
GDN_DV = 256
GDN_QK = GDN_H * GDN_DK
GDN_VW = GDN_H * GDN_DV
GDN_CONV = 2 * GDN_QK + GDN_VW
SBA_H = 16
SBA_HD = 64
SBA_W = SBA_H * SBA_HD
DEPTH = 4
DN_ALPHA = (2 * DEPTH) ** 0.25

LANES = 128
LOG2E = 1.4426950408889634
CHUNK = 128
VMEM_LIMIT = 56 * 1024 * 1024


def _cparams(sem):
    return pltpu.CompilerParams(dimension_semantics=sem, vmem_limit_bytes=VMEM_LIMIT)


def _dot(a, b):
    return jnp.dot(a, b, preferred_element_type=F32)


def _dot_nt(a, b):
    return lax.dot_general(a, b, (((1,), (1,)), ((), ())), preferred_element_type=F32)


def _split(x, terms):
    parts = []
    r = x
    for i in range(terms):
        p = r.astype(BF16)
        parts.append(p)
        if i + 1 < terms:
            r = r - p.astype(F32)
    return parts


def _dot_x01(x, e, terms=3):
    m = x.shape[0]
    parts = _split(x, terms)
    r = _dot(jnp.concatenate(parts, axis=0), e)
    out = r[0:m]
    for i in range(1, terms):
        out = out + r[i * m:(i + 1) * m]
    return out


def _dot_x01_stacked(x, e_stacked, terms=3):
    return _dot(jnp.concatenate(_split(x, terms), axis=1), e_stacked)


def _dot_01x(e, x, terms=3):
    n = x.shape[1]
    parts = _split(x, terms)
    r = _dot(e, jnp.concatenate(parts, axis=1))
    out = r[:, 0:n]
    for i in range(1, terms):
        out = out + r[:, i * n:(i + 1) * n]
    return out


def _sigmoid(x):
    return 1.0 / (1.0 + jnp.exp(-x))


def _silu(x):
    return x * _sigmoid(x)


def _softplus(x):
    return jnp.maximum(x, 0.0) + jnp.log1p(jnp.exp(-jnp.abs(x)))


def _tri(n, strict):
    r = lax.broadcasted_iota(jnp.int32, (n, n), 0)
    c = lax.broadcasted_iota(jnp.int32, (n, n), 1)
    return (c < r) if strict else (c <= r)


def _mm_kernel(x_ref, w_ref, o_ref, xb_ref):
    @pl.when(pl.program_id(1) == 0)
    def _():
        xb_ref[...] = x_ref[...].astype(BF16)

    o_ref[...] = _dot(xb_ref[...], w_ref[...]).astype(o_ref.dtype)


def _pick(n, cands):
    for c in cands:
        if n % c == 0:
            return c
    return n


def _matmul(x, w, name):
    m, k = x.shape
    n = w.shape[1]
    tm = _pick(m, (1536, 1024, 768, 512, 384, 256, 128))
    tn = _pick(n, (1024, 512, 256, 128))
    return pl.pallas_call(
        _mm_kernel,
        out_shape=jax.ShapeDtypeStruct((m, n), F32),
        grid=(m // tm, n // tn),
        in_specs=[pl.BlockSpec((tm, k), lambda i, j: (i, 0)),
                  pl.BlockSpec((k, tn), lambda i, j: (0, j))],
        out_specs=pl.BlockSpec((tm, tn), lambda i, j: (i, j)),
        scratch_shapes=[pltpu.VMEM((tm, k), BF16)],
        compiler_params=_cparams(("parallel", "arbitrary")),
        name=name,
    )(x, w)


def _out_ln_kernel(y_ref, w_ref, x_ref, g_ref, b_ref, o_ref):
    f = _dot(y_ref[...].astype(BF16), w_ref[...])
    r = DN_ALPHA * x_ref[...] + f
    mu = jnp.mean(r, axis=-1, keepdims=True)
    d = r - mu
    var = jnp.mean(d * d, axis=-1, keepdims=True)
    o_ref[...] = d * lax.rsqrt(var + EPS) * g_ref[...] + b_ref[...]


def _out_ln(y, w, x, g, b, name):
    m, k = y.shape
    d = w.shape[1]
    tm = _pick(m, (512, 384, 256, 128))
    return pl.pallas_call(
        _out_ln_kernel,
        out_shape=jax.ShapeDtypeStruct((m, d), F32),
        grid=(m // tm,),
        in_specs=[pl.BlockSpec((tm, k), lambda i: (i, 0)),
                  pl.BlockSpec((k, d), lambda i: (0, 0)),
                  pl.BlockSpec((tm, d), lambda i: (i, 0)),
                  pl.BlockSpec((1, d), lambda i: (0, 0)),
                  pl.BlockSpec((1, d), lambda i: (0, 0))],
        out_specs=pl.BlockSpec((tm, d), lambda i: (i, 0)),
        compiler_params=_cparams(("parallel",)),
        name=name,
    )(y, w, x, g.reshape(1, d), b.reshape(1, d))


def _conv_chunk(ext_ref, parts, pad, c, t, convw_ref, tail_ref, is_last):
    ncol = ext_ref.shape[1]

    @pl.when(c == 0)
    def _():
        ext_ref[0:8, :] = jnp.zeros((8, ncol), F32)

    @pl.when(c > 0)
    def _():
        ext_ref[0:8, :] = ext_ref[t:t + 8, :]

    off = 0
    for p in parts:
        w = p.shape[1]
        ext_ref[8:8 + t, off:off + w] = p[...]
        off += w

    @pl.when(is_last)
    def _():
        tail_ref[0] = ext_ref[t - pad + 5:t - pad + 8, :]

    full = ext_ref[...]
    acc = convw_ref[CONV_W - 1:CONV_W, :] * full[8:8 + t]
    for s in range(1, CONV_W):
        acc = acc + convw_ref[CONV_W - 1 - s:CONV_W - s, :] * pltpu.roll(full, s, axis=0)[8:8 + t]
    return acc


def _ssd_prompt_kernel(pad, z_ref, xs_ref, bc_ref, x_ref, wdt_ref, convw_ref, convb_ref,
                       dtb_ref, aneg_ref, dskip_ref, normw_ref, expand_ref,
                       yn_ref, hfin_ref, tail_ref, ext_ref, ht_ref, y_ref):
    c = pl.program_id(1)
    is_last = c == pl.num_programs(1) - 1
    t = z_ref.shape[0]
    gw = SSD_R * SSD_HD

    @pl.when(c == 0)
    def _():
        ht_ref[...] = jnp.zeros(ht_ref.shape, F32)

    row = lax.broadcasted_iota(jnp.int32, (t, 1), 0)
    valid = jnp.logical_or(jnp.logical_not(is_last), row < t - pad)

    conv = _conv_chunk(ext_ref, (xs_ref, bc_ref), pad, c, t, convw_ref, tail_ref, is_last)
    xbc = _silu(conv + convb_ref[...])
    xs = xbc[:, :SSD_DI]
    bm = xbc[:, SSD_DI:SSD_DI + SSD_G * SSD_N]
    cm = xbc[:, SSD_DI + SSD_G * SSD_N:]

    dt_raw = _dot(x_ref[...].astype(BF16), wdt_ref[...])
    dt = jnp.where(valid, _softplus(dt_raw + dtb_ref[...]), 0.0)
    a = dt * aneg_ref[...]
    ltri = jnp.where(_tri(t, False), 1.0, 0.0).astype(BF16)
    acum = _dot_01x(ltri, a)
    acum_t = acum.T

    both = _dot_x01_stacked(jnp.concatenate([dt, acum], axis=0), expand_ref[...])
    dt_e = both[0:t]
    acum_e = both[t:2 * t]
    last_e = acum_e[t - 1:t, :]
    eac_e = jnp.exp(acum_e)
    dte_e = jnp.exp(last_e - acum_e)
    cd_e = jnp.exp(last_e)

    xdt = xs * dt_e
    xw = xdt * dte_e
    tri = _tri(t, False)
    lane_g = lax.broadcasted_iota(jnp.int32, (1, gw), 1) // SSD_HD

    for g in range(SSD_G):
        gs = slice(g * gw, (g + 1) * gw)
        ns = slice(g * SSD_N, (g + 1) * SSD_N)
        bg = bm[:, ns]
        cg = cm[:, ns].astype(BF16)
        cb = _dot_nt(cg, bg.astype(BF16))
        xdt_g = xdt[:, gs].astype(BF16)
        ht = ht_ref[g]
        y_g = _dot(cg, ht.astype(BF16)) * eac_e[:, gs]
        for r in range(SSD_R):
            h = g * SSD_R + r
            diff = acum[:, h:h + 1] - acum_t[h:h + 1, :]
            seg = jnp.exp(jnp.where(tri, diff, -jnp.inf))
            yd = _dot((cb * seg).astype(BF16), xdt_g)
            y_g = y_g + jnp.where(lane_g == r, yd, 0.0)
        y_g = y_g + dskip_ref[:, gs] * xs[:, gs]
        y_ref[:, gs] = y_g
        ht_ref[g] = ht * cd_e[:, gs] + _dot(bg.T.astype(BF16), xw[:, gs].astype(BF16))

    y = y_ref[...] * _silu(z_ref[...])
    ms = jnp.mean(y * y, axis=-1, keepdims=True)
    yn_ref[...] = (y * lax.rsqrt(ms + EPS) * normw_ref[...]).astype(yn_ref.dtype)

    @pl.when(is_last)
    def _():
        for g in range(SSD_G):
            hfin_ref[0, g * gw:(g + 1) * gw, :] = ht_ref[g].T


def _ssd_consts(w):
    in_w, conv_w, conv_b, dt_bias, a_log, d_skip, norm_w, out_w = w
    main = in_w[:, :SSD_DI + SSD_CONV].astype(BF16)
    wdt = jnp.pad(in_w[:, SSD_DI + SSD_CONV:], ((0, 0), (0, LANES - SSD_H))).astype(BF16)
    dtb = jnp.pad(dt_bias.astype(F32), (0, LANES - SSD_H)).reshape(1, LANES)
    aneg = jnp.pad(-jnp.exp(a_log.astype(F32)), (0, LANES - SSD_H)).reshape(1, LANES)
    dskip_e = jnp.repeat(d_skip.astype(F32), SSD_HD).reshape(1, SSD_DI)
    hh = lax.broadcasted_iota(jnp.int32, (LANES, SSD_DI), 0)
    cc = lax.broadcasted_iota(jnp.int32, (LANES, SSD_DI), 1) // SSD_HD
    expand = (hh == cc).astype(BF16)
    return dict(main=main, wdt=wdt, dtb=dtb, aneg=aneg, dskip_e=dskip_e, expand=expand,
                expand3=jnp.concatenate([expand] * 3, axis=0),
                conv_w=conv_w.astype(F32), conv_b=conv_b.astype(F32).reshape(1, SSD_CONV),
                norm_w=norm_w.astype(F32).reshape(1, SSD_DI), out_w=out_w.astype(BF16))


def _ssd_prompt(xp, cs, pad, nb, lp, name):
    t = CHUNK
    nc = lp // t
    proj = _matmul(xp, cs["main"], name + "_in")
    const = lambda b, c: (0, 0)
    rows = lambda b, c: (b * nc + c, 0)
    yn, hfin, tail = pl.pallas_call(
        functools.partial(_ssd_prompt_kernel, pad),
        out_shape=(jax.ShapeDtypeStruct((nb * lp, SSD_DI), BF16),
                   jax.ShapeDtypeStruct((nb, SSD_DI, SSD_N), F32),
                   jax.ShapeDtypeStruct((nb, CONV_W - 1, SSD_CONV), F32)),
        grid=(nb, nc),
        in_specs=[pl.BlockSpec((t, SSD_DI), lambda b, c: (b * nc + c, 0)),
                  pl.BlockSpec((t, SSD_DI), lambda b, c: (b * nc + c, 1)),
                  pl.BlockSpec((t, SSD_DI), lambda b, c: (b * nc + c, 2)),
                  pl.BlockSpec((t, D_MODEL), rows),
                  pl.BlockSpec((D_MODEL, LANES), const),
                  pl.BlockSpec((CONV_W, SSD_CONV), const),
                  pl.BlockSpec((1, SSD_CONV), const),
                  pl.BlockSpec((1, LANES), const),
                  pl.BlockSpec((1, LANES), const),
                  pl.BlockSpec((1, SSD_DI), const),
                  pl.BlockSpec((1, SSD_DI), const),
                  pl.BlockSpec((3 * LANES, SSD_DI), const)],
        out_specs=(pl.BlockSpec((t, SSD_DI), rows),
                   pl.BlockSpec((1, SSD_DI, SSD_N), lambda b, c: (b, 0, 0)),
                   pl.BlockSpec((1, CONV_W - 1, SSD_CONV), lambda b, c: (b, 0, 0))),
        scratch_shapes=[pltpu.VMEM((t + 8, SSD_CONV), F32),
                        pltpu.VMEM((SSD_G, SSD_N, SSD_R * SSD_HD), F32),
                        pltpu.VMEM((t, SSD_DI), F32)],
        compiler_params=_cparams(("parallel", "arbitrary")),
        name=name,
    )(proj, proj, proj, xp, cs["wdt"], cs["conv_w"], cs["conv_b"], cs["dtb"], cs["aneg"],
      cs["dskip_e"], cs["norm_w"], cs["expand3"])
    return yn, hfin.reshape(nb, SSD_H, SSD_HD, SSD_N), tail


INV_BLOCK = 16


def _unit_lower_inverse(mats, t):
    n = range(len(mats))
    r = lax.broadcasted_iota(jnp.int32, (t, t), 0)
    c = lax.broadcasted_iota(jnp.int32, (t, t), 1)
    b = INV_BLOCK
    diag = (r // b) == (c // b)
    eye = jnp.where(r == c, 1.0, 0.0)
    ap = [jnp.where(diag, a, 0.0) for a in mats]
    p = [eye - ap[i] for i in n]
    for _ in range(int(math.log2(b)) - 1):
        apb = [ap[i].astype(BF16) for i in n]
        ap = [_dot(apb[i], apb[i]) for i in n]
        prod = [_dot(p[i].astype(BF16), ap[i].astype(BF16)) for i in n]
        p = [p[i] + prod[i] for i in n]
    while b < t:
        sel = jnp.logical_and((r // (2 * b)) == (c // (2 * b)), (r // b) != (c // b))
        off = [jnp.where(sel, a, 0.0).astype(BF16) for a in mats]
        pb = [p[i].astype(BF16) for i in n]
        left = [_dot(pb[i], off[i]).astype(BF16) for i in n]
        corr = [_dot(left[i], pb[i]) for i in n]
        p = [p[i] - corr[i] for i in n]
        b *= 2
    return p


def _gdn_prompt_kernel(pad, q_ref, k_ref, v_ref, z_ref, x_ref, wa_ref, wb_ref, convw_ref,
                       dtb_ref, aneg_ref, normw_ref,
                       on_ref, sfin_ref, tail_ref, ext_ref, s_ref):
    c = pl.program_id(1)
    is_last = c == pl.num_programs(1) - 1
    t = q_ref.shape[0]

    @pl.when(c == 0)
    def _():
        s_ref[...] = jnp.zeros(s_ref.shape, F32)

    row = lax.broadcasted_iota(jnp.int32, (t, 1), 0)
    valid = jnp.logical_or(jnp.logical_not(is_last), row < t - pad)

    conv = _conv_chunk(ext_ref, (q_ref, k_ref, v_ref), pad, c, t, convw_ref, tail_ref, is_last)
    qkv = _silu(conv)

    xb = x_ref[...].astype(BF16)
    a_raw = _dot(xb, wa_ref[...])
    b_raw = _dot(xb, wb_ref[...])
    gate = jnp.where(valid, aneg_ref[...] * _softplus(a_raw + dtb_ref[...]), 0.0)
    beta = jnp.where(valid, _sigmoid(b_raw), 0.0)
    ltri = jnp.where(_tri(t, False), 1.0, 0.0).astype(BF16)
    gc = _dot_01x(ltri, gate)
    gc_t = gc.T
    tri = _tri(t, False)
    strict = _tri(t, True)

    heads = range(GDN_H)
    q, k, v = [], [], []
    for h in heads:
        qh = qkv[:, h * GDN_DK:(h + 1) * GDN_DK]
        kh = qkv[:, GDN_QK + h * GDN_DK:GDN_QK + (h + 1) * GDN_DK]
        q.append(qh * lax.rsqrt(jnp.sum(qh * qh, axis=-1, keepdims=True) + 1e-6) * (GDN_DK ** -0.5))
        k.append(kh * lax.rsqrt(jnp.sum(kh * kh, axis=-1, keepdims=True) + 1e-6))
        v.append(qkv[:, 2 * GDN_QK + h * GDN_DV:2 * GDN_QK + (h + 1) * GDN_DV])
    gcol = [gc[:, h:h + 1] for h in heads]
    bcol = [beta[:, h:h + 1] for h in heads]
    dec = [jnp.exp(jnp.where(tri, gcol[h] - gc_t[h:h + 1, :], -jnp.inf)) for h in heads]
    eg = [jnp.exp(gcol[h]) for h in heads]
    kb = [k[h] * bcol[h] for h in heads]
    kbf = [k[h].astype(BF16) for h in heads]
    kk = [_dot_nt(kb[h].astype(BF16), kbf[h]) for h in heads]
    qk = [_dot_nt(q[h].astype(BF16), kbf[h]) for h in heads]
    a = [jnp.where(strict, kk[h] * dec[h], 0.0) for h in heads]
    qk = [(qk[h] * dec[h]).astype(BF16) for h in heads]
    tinv = [m.astype(BF16) for m in _unit_lower_inverse(a, t)]
    u = [_dot(tinv[h], (v[h] * bcol[h]).astype(BF16)) for h in heads]
    w = [_dot(tinv[h], (kb[h] * eg[h]).astype(BF16)) for h in heads]
    s = [s_ref[h] for h in heads]
    sb = [s[h].astype(BF16) for h in heads]
    ws = [_dot(w[h].astype(BF16), sb[h]) for h in heads]
    qs = [_dot((q[h] * eg[h]).astype(BF16), sb[h]) for h in heads]
    vnb = [(u[h] - ws[h]).astype(BF16) for h in heads]
    o = [qs[h] + _dot(qk[h], vnb[h]) for h in heads]
    gl = [gc[t - 1:t, h:h + 1] for h in heads]
    kd = [(k[h] * jnp.exp(gl[h] - gcol[h])).T.astype(BF16) for h in heads]
    upd = [_dot(kd[h], vnb[h]) for h in heads]
    for h in heads:
        s_ref[h] = s[h] * jnp.exp(gl[h]) + upd[h]
        ms = jnp.mean(o[h] * o[h], axis=-1, keepdims=True)
        zh = z_ref[:, h * GDN_DV:(h + 1) * GDN_DV]
        on_ref[:, h * GDN_DV:(h + 1) * GDN_DV] = (
            o[h] * lax.rsqrt(ms + EPS) * normw_ref[...] * _silu(zh)).astype(on_ref.dtype)

    @pl.when(is_last)
    def _():
        sfin_ref[0] = s_ref[...]


def _gdn_consts(w):
    in_w, conv_w, dt_bias, a_log, norm_w, out_w = w
    nmain = GDN_CONV + GDN_VW
    main = in_w[:, :nmain].astype(BF16)
    wa = jnp.pad(in_w[:, nmain:nmain + GDN_H], ((0, 0), (0, LANES - GDN_H))).astype(BF16)
    wb = jnp.pad(in_w[:, nmain + GDN_H:], ((0, 0), (0, LANES - GDN_H))).astype(BF16)
    dtb = jnp.pad(dt_bias.astype(F32), (0, LANES - GDN_H)).reshape(1, LANES)
    aneg = jnp.pad(-jnp.exp(a_log.astype(F32)), (0, LANES - GDN_H)).reshape(1, LANES)
    return dict(main=main, wa=wa, wb=wb, dtb=dtb, aneg=aneg, conv_w=conv_w.astype(F32),
                norm_w=norm_w.astype(F32).reshape(1, GDN_DV), out_w=out_w.astype(BF16))


def _gdn_prompt(xp, cs, pad, nb, lp, name):
    t = CHUNK
    nc = lp // t
    proj = _matmul(xp, cs["main"], name + "_in")
    const = lambda b, c: (0, 0)
    rows = lambda b, c: (b * nc + c, 0)
    on, sfin, tail = pl.pallas_call(
        functools.partial(_gdn_prompt_kernel, pad),
        out_shape=(jax.ShapeDtypeStruct((nb * lp, GDN_VW), BF16),
                   jax.ShapeDtypeStruct((nb, GDN_H, GDN_DK, GDN_DV), F32),
                   jax.ShapeDtypeStruct((nb, CONV_W - 1, GDN_CONV), F32)),
        grid=(nb, nc),
        in_specs=[pl.BlockSpec((t, GDN_QK), lambda b, c: (b * nc + c, 0)),
                  pl.BlockSpec((t, GDN_QK), lambda b, c: (b * nc + c, 1)),
                  pl.BlockSpec((t, GDN_VW), lambda b, c: (b * nc + c, 1)),
                  pl.BlockSpec((t, GDN_VW), lambda b, c: (b * nc + c, 2)),
                  pl.BlockSpec((t, D_MODEL), rows),
                  pl.BlockSpec((D_MODEL, LANES), const),
                  pl.BlockSpec((D_MODEL, LANES), const),
                  pl.BlockSpec((CONV_W, GDN_CONV), const),
                  pl.BlockSpec((1, LANES), const),
                  pl.BlockSpec((1, LANES), const),
                  pl.BlockSpec((1, GDN_DV), const)],
        out_specs=(pl.BlockSpec((t, GDN_VW), rows),
                   pl.BlockSpec((1, GDN_H, GDN_DK, GDN_DV), lambda b, c: (b, 0, 0, 0)),
                   pl.BlockSpec((1, CONV_W - 1, GDN_CONV), lambda b, c: (b, 0, 0))),
        scratch_shapes=[pltpu.VMEM((t + 8, GDN_CONV), F32),
                        pltpu.VMEM((GDN_H, GDN_DK, GDN_DV), F32)],
        compiler_params=_cparams(("parallel", "arbitrary")),
        name=name,
    )(proj, proj, proj, proj, xp, cs["wa"], cs["wb"], cs["conv_w"], cs["dtb"], cs["aneg"],
      cs["norm_w"])
    return on, sfin, tail


SBA_GROUP = 8


def _sba_prompt_kernel(bias_ref, q_ref, k_ref, v_ref, z_ref, y_ref, acc_ref, car_ref):
    hg = pl.program_id(1)
    qi = pl.program_id(2)
    t = q_ref.shape[0]
    hw = 2 * SBA_HD
    nheads = acc_ref.shape[0]
    first = lax.broadcasted_iota(jnp.int32, (1, hw), 1) < SBA_HD
    qh = []
    for p in range(nheads // 2):
        q = q_ref[:, p * hw:(p + 1) * hw] * (SBA_HD ** -0.5 * LOG2E)
        qh.append(jnp.where(first, q, 0.0).astype(BF16))
        qh.append(jnp.where(first, 0.0, q).astype(BF16))
    bias = [bias_ref[hg * nheads + h] * LOG2E for h in range(nheads)]
    r = lax.broadcasted_iota(jnp.int32, (2 * t, 2 * t), 0) % t
    c = lax.broadcasted_iota(jnp.int32, (2 * t, 2 * t), 1)
    later_ones = jnp.where(jnp.logical_or(c < r, c >= t), 1.0, 0.0).astype(BF16)
    acc_ref[...] = jnp.zeros(acc_ref.shape, F32)
    car_ref[...] = jnp.zeros(car_ref.shape, F32)

    def block(js, vis):
        heads = range(nheads)
        nj = range(len(js))
        starts = [pl.multiple_of(j * t, t) for j in js]
        kb = [[k_ref[pl.ds(st, t), p * hw:(p + 1) * hw].astype(BF16) for p in range(nheads // 2)]
              for st in starts]
        vb = [[v_ref[pl.ds(st, t), p * hw:(p + 1) * hw].astype(BF16) for p in range(nheads // 2)]
              for st in starts]
        z = [[_dot_nt(qh[h], kb[i][h // 2]) + bias[h] for h in heads] for i in nj]
        sp = [[jnp.maximum(z[i][h], 0.0) + jnp.log2(1.0 + jnp.exp2(-jnp.abs(z[i][h]))) for h in heads]
              for i in nj]
        cat = []
        for i in nj:
            row = []
            for h in heads:
                lk = -sp[i][h] if vis is None else jnp.where(vis, -sp[i][h], 0.0)
                hi = lk.astype(BF16)
                row.append(jnp.concatenate([hi, (lk - hi.astype(F32)).astype(BF16)], axis=1))
            cat.append(row)
        res = [[_dot(cat[i][h], later_ones) for h in heads] for i in nj]
        car = [car_ref[h] for h in heads]
        a = []
        for i in nj:
            row = []
            for h in heads:
                ah = jnp.exp2(z[i][h] - sp[i][h] + (res[i][h][:, 0:t] + car[h]))
                row.append((ah if vis is None else jnp.where(vis, ah, 0.0)).astype(BF16))
                car[h] = car[h] + res[i][h][:, t:2 * t]
            a.append(row)
        o = [[_dot(a[i][h], vb[i][h // 2]) for h in heads] for i in nj]
        for h in heads:
            tot = o[0][h]
            for i in nj[1:]:
                tot = tot + o[i][h]
            acc_ref[h] += tot
            car_ref[h] = car[h]

    qrow = lax.broadcasted_iota(jnp.int32, (t, t), 0)
    klane = lax.broadcasted_iota(jnp.int32, (t, t), 1)
    block([qi], klane < qrow)

    def body(jj, carry):
        j = qi - 1 - 2 * jj
        block([j, j - 1], None)
        return carry

    lax.fori_loop(0, qi // 2, body, 0)

    @pl.when(qi % 2 == 1)
    def _():
        block([0], None)

    for p in range(nheads // 2):
        o = jnp.where(first, acc_ref[2 * p], acc_ref[2 * p + 1])
        y_ref[:, p * hw:(p + 1) * hw] = (o * _silu(z_ref[:, p * hw:(p + 1) * hw])).astype(y_ref.dtype)


def _sba_prompt(xp, in_w, sb_bias, nb, lp, name):
    t = CHUNK
    nq = lp // t
    gw = SBA_GROUP * SBA_HD
    ngrp = SBA_H // SBA_GROUP
    proj = _matmul(xp, in_w, name + "_in")
    y = pl.pallas_call(
        _sba_prompt_kernel,
        out_shape=jax.ShapeDtypeStruct((nb * lp, SBA_W), BF16),
        grid=(nb, ngrp, nq),
        in_specs=[pl.BlockSpec(memory_space=pltpu.SMEM),
                  pl.BlockSpec((t, gw), lambda b, p, i: (b * nq + i, p)),
                  pl.BlockSpec((lp, gw), lambda b, p, i: (b, ngrp + p)),
                  pl.BlockSpec((lp, gw), lambda b, p, i: (b, 2 * ngrp + p)),
                  pl.BlockSpec((t, gw), lambda b, p, i: (b * nq + i, 3 * ngrp + p))],
        out_specs=pl.BlockSpec((t, gw), lambda b, p, i: (b * nq + i, p)),
        scratch_shapes=[pltpu.VMEM((SBA_GROUP, t, 2 * SBA_HD), F32),
                        pltpu.VMEM((SBA_GROUP, t, t), F32)],
        compiler_params=_cparams(("parallel", "parallel", "arbitrary")),
        name=name,
    )(sb_bias.astype(F32), proj, proj, proj, proj)
    return y, proj


DEC_ROWS = 32
DEC_SEQS = 8


def _dec_conv(parts, st_ref, convw_ref, newst_ref):
    raw = jnp.concatenate([p[...] for p in parts], axis=1)
    conv = convw_ref[CONV_W - 1:CONV_W, :] * raw
    for j in range(CONV_W - 1):
        conv = conv + convw_ref[j:j + 1, :] * st_ref[j]
    for j in range(CONV_W - 2):
        newst_ref[j] = st_ref[j + 1]
    newst_ref[CONV_W - 2] = raw
    return conv


def _ssd_dec_pre_kernel(xs_ref, bc_ref, x_ref, st_ref, wdt_ref, convw_ref, convb_ref, dtb_ref,
                        aneg_ref, expand_ref, newst_ref, xsa_ref, xdt_ref, bca_ref, dec_ref):
    conv = _dec_conv((xs_ref, bc_ref), st_ref, convw_ref, newst_ref)
    xbc = _silu(conv + convb_ref[...])
    dt = _softplus(_dot(x_ref[...].astype(BF16), wdt_ref[...]) + dtb_ref[...])
    dec_ref[...] = jnp.exp(dt * aneg_ref[...])
    xsa = xbc[:, :SSD_DI]
    xsa_ref[...] = xsa
    xdt_ref[...] = xsa * _dot_x01(dt, expand_ref[...])
    bca_ref[...] = xbc[:, SSD_DI:]


def _ssd_dec_state_kernel(dec_ref, h_ref, xdt_ref, bm_ref, cm_ref, xsa_ref, z_ref, dskip_ref,
                          normw_ref, hout_ref, yn_ref):
    blk = pl.program_id(0)
    nseq = h_ref.shape[0]
    gw = SSD_R * SSD_HD
    nrow = nseq * SSD_G
    gmask = (lax.broadcasted_iota(jnp.int32, (SSD_G, SSD_DI), 1) // gw
             == lax.broadcasted_iota(jnp.int32, (SSD_G, SSD_DI), 0))
    pieces = [jnp.where(gmask, jnp.broadcast_to(xdt_ref[i:i + 1, :], (SSD_G, SSD_DI)), 0.0)
              for i in range(nseq)]
    pieces.append(jnp.zeros((LANES - nrow, SSD_DI), F32))
    a_t = jnp.concatenate(pieces, axis=0).T.astype(BF16)
    bm = jnp.concatenate([bm_ref[...], jnp.zeros((LANES - nrow, SSD_N), F32)], axis=0)
    rid = lax.broadcasted_iota(jnp.int32, (LANES, 1), 0) // SSD_G
    y_rows = []
    for i in range(nseq):
        outer = _dot(a_t, jnp.where(rid == i, bm, 0.0).astype(BF16))
        c_i = cm_ref[i * SSD_G:(i + 1) * SSD_G, :].astype(BF16)
        y_parts = []
        for g in range(SSD_G):
            hn = []
            for r in range(SSD_R):
                h = g * SSD_R + r
                hs = slice(h * SSD_HD, (h + 1) * SSD_HD)
                d = dec_ref[(blk * nseq + i) * SSD_H + h]
                v = h_ref[i, hs, :] * d + outer[hs, :]
                hout_ref[i, hs, :] = v
                hn.append(v)
            hg = jnp.concatenate(hn, axis=0).astype(BF16)
            y_parts.append(_dot_nt(c_i, hg)[g:g + 1, :])
        y_rows.append(jnp.concatenate(y_parts, axis=1))
    y = jnp.concatenate(y_rows, axis=0) + dskip_ref[...] * xsa_ref[...]
    y = y * _silu(z_ref[...])
    ms = jnp.mean(y * y, axis=-1, keepdims=True)
    yn_ref[...] = y * lax.rsqrt(ms + EPS) * normw_ref[...]


def _ssd_decode(xs, state, conv_state, cs, name):
    db = xs.shape[0]
    rb = _pick(db, (DEC_ROWS, 8))
    proj = _matmul(xs, cs["main"], name + "_in")
    const = lambda i: (0, 0)
    ntap = CONV_W - 1
    st_spec = pl.BlockSpec((ntap, rb, SSD_CONV), lambda i: (0, i, 0))
    newst, xsa, xdt, bca, dec = pl.pallas_call(
        _ssd_dec_pre_kernel,
        out_shape=(jax.ShapeDtypeStruct((ntap, db, SSD_CONV), F32),
                   jax.ShapeDtypeStruct((db, SSD_DI), F32),
                   jax.ShapeDtypeStruct((db, SSD_DI), F32),
                   jax.ShapeDtypeStruct((db, 2 * SSD_G * SSD_N), F32),
                   jax.ShapeDtypeStruct((db, LANES), F32)),
        grid=(db // rb,),
        in_specs=[pl.BlockSpec((rb, SSD_DI), lambda i: (i, 1)),
                  pl.BlockSpec((rb, SSD_DI), lambda i: (i, 2)),
                  pl.BlockSpec((rb, D_MODEL), lambda i: (i, 0)),
                  st_spec,
                  pl.BlockSpec((D_MODEL, LANES), const),
                  pl.BlockSpec((CONV_W, SSD_CONV), const),
                  pl.BlockSpec((1, SSD_CONV), const),
                  pl.BlockSpec((1, LANES), const),
                  pl.BlockSpec((1, LANES), const),
                  pl.BlockSpec((LANES, SSD_DI), const)],
        out_specs=(st_spec,
                   pl.BlockSpec((rb, SSD_DI), lambda i: (i, 0)),
                   pl.BlockSpec((rb, SSD_DI), lambda i: (i, 0)),
                   pl.BlockSpec((rb, 2 * SSD_G * SSD_N), lambda i: (i, 0)),
                   pl.BlockSpec((rb, LANES), lambda i: (i, 0))),
        compiler_params=_cparams(("parallel",)),
        name=name + "_pre",
    )(proj, proj, xs, jnp.transpose(conv_state, (1, 0, 2)), cs["wdt"], cs["conv_w"], cs["conv_b"],
      cs["dtb"], cs["aneg"], cs["expand"])
    ns = DEC_SEQS
    bm = bca[:, :SSD_G * SSD_N].reshape(db * SSD_G, SSD_N)
    cm = bca[:, SSD_G * SSD_N:].reshape(db * SSD_G, SSD_N)
    hout, yn = pl.pallas_call(
        _ssd_dec_state_kernel,
        out_shape=(jax.ShapeDtypeStruct((db, SSD_DI, SSD_N), F32),
                   jax.ShapeDtypeStruct((db, SSD_DI), F32)),
        grid=(db // ns,),
        in_specs=[pl.BlockSpec(memory_space=pltpu.SMEM),
                  pl.BlockSpec((ns, SSD_DI, SSD_N), lambda i: (i, 0, 0)),
                  pl.BlockSpec((ns, SSD_DI), lambda i: (i, 0)),
                  pl.BlockSpec((ns * SSD_G, SSD_N), lambda i: (i, 0)),
                  pl.BlockSpec((ns * SSD_G, SSD_N), lambda i: (i, 0)),
                  pl.BlockSpec((ns, SSD_DI), lambda i: (i, 0)),
                  pl.BlockSpec((ns, SSD_DI), lambda i: (i, 0)),
                  pl.BlockSpec((1, SSD_DI), const),
                  pl.BlockSpec((1, SSD_DI), const)],
        out_specs=(pl.BlockSpec((ns, SSD_DI, SSD_N), lambda i: (i, 0, 0)),
                   pl.BlockSpec((ns, SSD_DI), lambda i: (i, 0))),
        compiler_params=_cparams(("parallel",)),
        name=name + "_state",
    )(dec[:, :SSD_H].reshape(db * SSD_H), state.reshape(db, SSD_DI, SSD_N), xdt, bm, cm, xsa,
      proj, cs["dskip_e"], cs["norm_w"])
    return (yn, hout.reshape(db, SSD_H, SSD_HD, SSD_N),
            jnp.transpose(newst, (1, 0, 2)))


def _gdn_dec_pre_kernel(q_ref, k_ref, v_ref, x_ref, st_ref, wa_ref, wb_ref, convw_ref, dtb_ref,
                        aneg_ref, newst_ref, qn_ref, kn_ref, va_ref, eg_ref, beta_ref, qk_ref):
    conv = _dec_conv((q_ref, k_ref, v_ref), st_ref, convw_ref, newst_ref)
    qkv = _silu(conv)
    xb = x_ref[...].astype(BF16)
    eg_ref[...] = jnp.exp(aneg_ref[...] * _softplus(_dot(xb, wa_ref[...]) + dtb_ref[...]))
    beta_ref[...] = _sigmoid(_dot(xb, wb_ref[...]))
    lane = lax.broadcasted_iota(jnp.int32, (1, LANES), 1)
    qk = jnp.zeros(qk_ref.shape, F32)
    for h in range(GDN_H):
        hs = slice(h * GDN_DK, (h + 1) * GDN_DK)
        q = qkv[:, hs]
        k = qkv[:, GDN_QK + h * GDN_DK:GDN_QK + (h + 1) * GDN_DK]
        q = q * lax.rsqrt(jnp.sum(q * q, axis=-1, keepdims=True) + 1e-6) * (GDN_DK ** -0.5)
        k = k * lax.rsqrt(jnp.sum(k * k, axis=-1, keepdims=True) + 1e-6)
        qn_ref[:, hs] = q
        kn_ref[:, hs] = k
        qk = jnp.where(lane == h, jnp.sum(q * k, axis=-1, keepdims=True), qk)
    qk_ref[...] = qk
    va_ref[...] = qkv[:, 2 * GDN_QK:]


def _gdn_dec_state_kernel(eg_ref, beta_ref, qk_ref, s_ref, q_ref, k_ref, v_ref, z_ref, normw_ref,
                          sout_ref, on_ref):
    blk = pl.program_id(0)
    nseq = s_ref.shape[0]
    nrow = nseq * GDN_H
    kparts = [k_ref[:, h * GDN_DK:(h + 1) * GDN_DK] for h in range(GDN_H)]
    kparts.append(jnp.zeros((LANES - nrow, GDN_DK), F32))
    k_t = jnp.concatenate(kparts, axis=0).T.astype(BF16)
    rid = lax.broadcasted_iota(jnp.int32, (LANES, 1), 0)
    o_cols = []
    for h in range(GDN_H):
        hs = slice(h * GDN_DK, (h + 1) * GDN_DK)
        vs = slice(h * GDN_DV, (h + 1) * GDN_DV)
        kq = jnp.concatenate([k_ref[:, hs], q_ref[:, hs]], axis=0).astype(BF16)
        o_rows = []
        for i in range(nseq):
            sidx = (blk * nseq + i) * GDN_H + h
            eg = eg_ref[sidx]
            s = s_ref[i, h]
            r = _dot(kq, s.astype(BF16))
            v_new = beta_ref[sidx] * (v_ref[i:i + 1, vs] - eg * r[i:i + 1, :])
            o_rows.append(eg * r[nseq + i:nseq + i + 1, :] + qk_ref[sidx] * v_new)
            rm = jnp.where(rid == h * nseq + i, jnp.broadcast_to(v_new, (LANES, GDN_DV)), 0.0)
            sout_ref[i, h] = s * eg + _dot(k_t, rm.astype(BF16))
        o = jnp.concatenate(o_rows, axis=0)
        ms = jnp.mean(o * o, axis=-1, keepdims=True)
        o_cols.append(o * lax.rsqrt(ms + EPS) * normw_ref[...] * _silu(z_ref[:, vs]))
    on_ref[...] = jnp.concatenate(o_cols, axis=1)


def _gdn_decode(xs, state, conv_state, cs, name):
    db = xs.shape[0]
    rb = _pick(db, (DEC_ROWS, 8))
    proj = _matmul(xs, cs["main"], name + "_in")
    const = lambda i: (0, 0)
    ntap = CONV_W - 1
    st_spec = pl.BlockSpec((ntap, rb, GDN_CONV), lambda i: (0, i, 0))
    row = lambda w: pl.BlockSpec((rb, w), lambda i: (i, 0))
    newst, qn, kn, va, eg, beta, qk = pl.pallas_call(
        _gdn_dec_pre_kernel,
        out_shape=(jax.ShapeDtypeStruct((ntap, db, GDN_CONV), F32),
                   jax.ShapeDtypeStruct((db, GDN_QK), F32),
                   jax.ShapeDtypeStruct((db, GDN_QK), F32),
                   jax.ShapeDtypeStruct((db, GDN_VW), F32),
                   jax.ShapeDtypeStruct((db, LANES), F32),
                   jax.ShapeDtypeStruct((db, LANES), F32),
                   jax.ShapeDtypeStruct((db, LANES), F32)),
        grid=(db // rb,),
        in_specs=[pl.BlockSpec((rb, GDN_QK), lambda i: (i, 0)),
                  pl.BlockSpec((rb, GDN_QK), lambda i: (i, 1)),
                  pl.BlockSpec((rb, GDN_VW), lambda i: (i, 1)),
                  row(D_MODEL), st_spec,
                  pl.BlockSpec((D_MODEL, LANES), const),
                  pl.BlockSpec((D_MODEL, LANES), const),
                  pl.BlockSpec((CONV_W, GDN_CONV), const),
                  pl.BlockSpec((1, LANES), const),
                  pl.BlockSpec((1, LANES), const)],
        out_specs=(st_spec, row(GDN_QK), row(GDN_QK), row(GDN_VW), row(LANES), row(LANES),
                   row(LANES)),
        compiler_params=_cparams(("parallel",)),
        name=name + "_pre",
    )(proj, proj, proj, xs, jnp.transpose(conv_state, (1, 0, 2)), cs["wa"], cs["wb"], cs["conv_w"],
      cs["dtb"], cs["aneg"])
    ns = DEC_SEQS
    smem = pl.BlockSpec(memory_space=pltpu.SMEM)
    flat = lambda a: a[:, :GDN_H].reshape(db * GDN_H)
    sout, on = pl.pallas_call(
        _gdn_dec_state_kernel,
        out_shape=(jax.ShapeDtypeStruct((db, GDN_H, GDN_DK, GDN_DV), F32),
                   jax.ShapeDtypeStruct((db, GDN_VW), F32)),
        grid=(db // ns,),
        in_specs=[smem, smem, smem,
                  pl.BlockSpec((ns, GDN_H, GDN_DK, GDN_DV), lambda i: (i, 0, 0, 0)),
                  pl.BlockSpec((ns, GDN_QK), lambda i: (i, 0)),
                  pl.BlockSpec((ns, GDN_QK), lambda i: (i, 0)),
                  pl.BlockSpec((ns, GDN_VW), lambda i: (i, 0)),
                  pl.BlockSpec((ns, GDN_VW), lambda i: (i, 2)),
                  pl.BlockSpec((1, GDN_DV), const)],
        out_specs=(pl.BlockSpec((ns, GDN_H, GDN_DK, GDN_DV), lambda i: (i, 0, 0, 0)),
                   pl.BlockSpec((ns, GDN_VW), lambda i: (i, 0))),
        compiler_params=_cparams(("parallel",)),
        name=name + "_state",
    )(flat(eg), flat(beta), flat(qk), state, qn, kn, va, proj, cs["norm_w"])
    return on, sout, jnp.transpose(newst, (1, 0, 2))


DEC_PAGES = 8


def _sba_dec_kernel(pt_ref, q_ref, zg_ref, bias_ref, *rest):
    npg = DEC_PAGES
    k_refs = rest[:npg]
    v_refs = rest[npg:2 * npg]
    y_ref, qb_ref, acc_ref, car_ref = rest[2 * npg:]
    s = pl.program_id(1)
    ps = k_refs[0].shape[3]
    rid = lax.broadcasted_iota(jnp.int32, (SBA_H, 1), 0)

    @pl.when(s == 0)
    def _():
        acc_ref[...] = jnp.zeros(acc_ref.shape, F32)
        car_ref[...] = jnp.zeros(car_ref.shape, F32)
        q = q_ref[0] * (SBA_HD ** -0.5 * LOG2E)
        q = jnp.concatenate([q, jnp.zeros((SBA_H, LANES - SBA_HD), F32)], axis=1)
        qt = jnp.concatenate([q, jnp.zeros((LANES - SBA_H, LANES), F32)], axis=0).T
        for h in range(SBA_H):
            qb_ref[h] = jnp.broadcast_to(qt[0:SBA_HD, h:h + 1], (SBA_HD, ps))

    bias = bias_ref[...] * LOG2E
    r = lax.broadcasted_iota(jnp.int32, (2 * ps, 2 * ps), 0) % ps
    c = lax.broadcasted_iota(jnp.int32, (2 * ps, 2 * ps), 1)
    later_ones = jnp.where(jnp.logical_or(r > c, c >= ps), 1.0, 0.0).astype(BF16)

    pages = range(npg)
    z, sp, cat = [], [], []
    for p in pages:
        zp = bias
        for h in range(SBA_H):
            zh = jnp.sum(k_refs[p][0, h] * qb_ref[h], axis=0, keepdims=True)
            zp = zp + jnp.where(rid == h, zh, 0.0)
        spp = jnp.maximum(zp, 0.0) + jnp.log2(1.0 + jnp.exp2(-jnp.abs(zp)))
        hi = (-spp).astype(BF16)
        cat.append(jnp.concatenate([hi, (-spp - hi.astype(F32)).astype(BF16)], axis=1))
        z.append(zp)
        sp.append(spp)
    res = _dot(jnp.concatenate(cat, axis=0), later_ones)
    car = car_ref[...]
    a = []
    for p in pages:
        rp = res[p * SBA_H:(p + 1) * SBA_H]
        a.append(jnp.exp2(z[p] - sp[p] + (rp[:, 0:ps] + car)))
        car = car + rp[:, ps:2 * ps]
    car_ref[...] = car
    for h in range(SBA_H):
        t = acc_ref[h]
        for p in pages:
            t = t + v_refs[p][0, h] * a[p][h:h + 1, :]
        acc_ref[h] = t

    @pl.when(s == pl.num_programs(1) - 1)
    def _():
        ones = jnp.ones((8, ps), BF16)
        o = jnp.zeros((SBA_H, SBA_HD), F32)
        for h in range(SBA_H):
            parts = _split(acc_ref[h], 3)
            oh = _dot_nt(ones, parts[0]) + _dot_nt(ones, parts[1]) + _dot_nt(ones, parts[2])
            o = o + jnp.where(rid == h, oh[0:1, :], 0.0)
        y_ref[0] = o * _silu(zg_ref[0])


def _sba_decode(xs, cache_k, cache_v, page_table, in_w, sb_bias, name):
    db = xs.shape[0]
    ps = cache_k.shape[1]
    npages = page_table.shape[1]
    npg = DEC_PAGES
    assert npages % npg == 0 and ps == LANES
    proj = _matmul(xs, in_w, name + "_in")
    q = proj[:, :SBA_W].reshape(db, SBA_H, SBA_HD)
    zg = proj[:, 3 * SBA_W:].reshape(db, SBA_H, SBA_HD)
    bias = jnp.broadcast_to(sb_bias.astype(F32)[:, None], (SBA_H, LANES))
    kpool = jnp.transpose(cache_k, (0, 2, 3, 1))
    vpool = jnp.transpose(cache_v, (0, 2, 3, 1))

    def page_spec(r):
        return pl.BlockSpec((1, SBA_H, SBA_HD, ps),
                            lambda b, s, pt: (pt[b * npages + npages - 1 - (s * npg + r)], 0, 0, 0))

    one = pl.BlockSpec((1, SBA_H, SBA_HD), lambda b, s, pt: (b, 0, 0))
    y = pl.pallas_call(
        _sba_dec_kernel,
        out_shape=jax.ShapeDtypeStruct((db, SBA_H, SBA_HD), F32),
        grid_spec=pltpu.PrefetchScalarGridSpec(
            num_scalar_prefetch=1,
            grid=(db, npages // npg),
            in_specs=[one, one, pl.BlockSpec((SBA_H, LANES), lambda b, s, pt: (0, 0))]
                     + [page_spec(r) for r in range(npg)] + [page_spec(r) for r in range(npg)],
            out_specs=one,
            scratch_shapes=[pltpu.VMEM((SBA_H, SBA_HD, ps), F32),
                            pltpu.VMEM((SBA_H, SBA_HD, ps), F32),
                            pltpu.VMEM((SBA_H, LANES), F32)]),
        compiler_params=_cparams(("parallel", "arbitrary")),
        name=name,
    )(page_table.reshape(-1).astype(jnp.int32), q, zg, bias, *([kpool] * npg), *([vpool] * npg))
    return y.reshape(db, SBA_W), proj


def _pad_prompt(x_prompt, meta_tokens):
    nb, seq, d = x_prompt.shape
    n_meta = meta_tokens.shape[0]
    real = n_meta + seq
    pad = (-real) % CHUNK
    lp = pad + real
    meta = jnp.broadcast_to(meta_tokens[None].astype(x_prompt.dtype), (nb, n_meta, d))
    assert CHUNK - pad >= CONV_W - 1
    xp = jnp.concatenate([meta, x_prompt, jnp.zeros((nb, pad, d), x_prompt.dtype)], axis=1)
    return xp.reshape(nb * lp, d), pad, lp


def kernel(x_prompt, x_sample, state_l0_ssm, state_l0_conv, state_l1_delta, state_l1_conv, cache_l2_k, cache_l2_v, state_l3_ssm, state_l3_conv, page_table, meta_tokens, l0_in_w, l0_conv_w, l0_conv_b, l0_dt_bias, l0_a_log, l0_d_skip, l0_norm_w, l0_out_w, l0_ln_g, l0_ln_b, l1_in_w, l1_conv_w, l1_dt_bias, l1_a_log, l1_norm_w, l1_out_w, l1_ln_g, l1_ln_b, l2_in_w, l2_sb_bias, l2_out_w, l2_ln_g, l2_ln_b, l3_in_w, l3_conv_w, l3_conv_b, l3_dt_bias, l3_a_log, l3_d_skip, l3_norm_w, l3_out_w, l3_ln_g, l3_ln_b):
    nb, seq, d = x_prompt.shape
    db = x_sample.shape[0]
    assert x_sample.shape[1] == 1 and d == D_MODEL
    xp, pad, lp = _pad_prompt(x_prompt, meta_tokens)
    real = lp - pad
    xs = x_sample.reshape(db, d)

    def ssd_layer(xp, xs, w, ln_g, ln_b, state, conv_state, name):
        cs = _ssd_consts(w)
        yn, p_ssm, p_conv = _ssd_prompt(xp, cs, pad, nb, lp, name + "_prompt")
        xp = _out_ln(yn, cs["out_w"], xp, ln_g, ln_b, name + "_prompt_out")
        yd, s_ssm, s_conv = _ssd_decode(xs, state, conv_state, cs, name + "_dec")
        xs = _out_ln(yd, cs["out_w"], xs, ln_g, ln_b, name + "_dec_out")
        return xp, xs, p_ssm, p_conv, s_ssm, s_conv.reshape(db, CONV_W - 1, SSD_CONV)

    xp, xs, p_l0_ssm, p_l0_conv, s_l0_ssm, s_l0_conv = ssd_layer(
        xp, xs, (l0_in_w, l0_conv_w, l0_conv_b, l0_dt_bias, l0_a_log, l0_d_skip, l0_norm_w, l0_out_w),
        l0_ln_g, l0_ln_b, state_l0_ssm, state_l0_conv, "l0")

    cs1 = _gdn_consts((l1_in_w, l1_conv_w, l1_dt_bias, l1_a_log, l1_norm_w, l1_out_w))
    on, p_l1_delta, p_l1_conv = _gdn_prompt(xp, cs1, pad, nb, lp, "l1_prompt")
    xp = _out_ln(on, cs1["out_w"], xp, l1_ln_g, l1_ln_b, "l1_prompt_out")
    od, s_l1_delta, s_l1_conv = _gdn_decode(xs, state_l1_delta, state_l1_conv, cs1, "l1_dec")
    xs = _out_ln(od, cs1["out_w"], xs, l1_ln_g, l1_ln_b, "l1_dec_out")

    w2_in = l2_in_w.astype(BF16)
    w2_out = l2_out_w.astype(BF16)
    ya, proj_p = _sba_prompt(xp, w2_in, l2_sb_bias, nb, lp, "l2_prompt")
    xp = _out_ln(ya, w2_out, xp, l2_ln_g, l2_ln_b, "l2_prompt_out")
    proj_p = proj_p.reshape(nb, lp, 4 * SBA_W)[:, :real]
    p_l2_k = proj_p[..., SBA_W:2 * SBA_W].reshape(nb, real, SBA_H, SBA_HD)
    p_l2_v = proj_p[..., 2 * SBA_W:3 * SBA_W].reshape(nb, real, SBA_H, SBA_HD)
    yd, proj_s = _sba_decode(xs, cache_l2_k, cache_l2_v, page_table, w2_in, l2_sb_bias, "l2_dec")
    xs = _out_ln(yd, w2_out, xs, l2_ln_g, l2_ln_b, "l2_dec_out")
    s_l2_k = proj_s[:, SBA_W:2 * SBA_W].reshape(db, 1, SBA_H, SBA_HD)
    s_l2_v = proj_s[:, 2 * SBA_W:3 * SBA_W].reshape(db, 1, SBA_H, SBA_HD)

    xp, xs, p_l3_ssm, p_l3_conv, s_l3_ssm, s_l3_conv = ssd_layer(
        xp, xs, (l3_in_w, l3_conv_w, l3_conv_b, l3_dt_bias, l3_a_log, l3_d_skip, l3_norm_w, l3_out_w),
        l3_ln_g, l3_ln_b, state_l3_ssm, state_l3_conv, "l3")

    y_prompt = xp.reshape(nb, lp, d)[:, meta_tokens.shape[0]:real]
    y_sample = xs.reshape(db, 1, d)
    return (y_prompt, y_sample,
            p_l0_ssm, p_l0_conv, p_l1_delta, p_l1_conv, p_l2_k, p_l2_v, p_l3_ssm, p_l3_conv,
            s_l0_ssm, s_l0_conv, s_l1_delta, s_l1_conv, s_l2_k, s_l2_v, s_l3_ssm, s_l3_conv)
```

```python
import functools
import math

import jax
import jax.numpy as jnp
from jax import lax
from jax.experimental import pallas as pl
from jax.experimental.pallas import tpu as pltpu

F32 = jnp.float32
BF16 = jnp.bfloat16

N_META = 16
CONV_W = 4
EPS = 1e-5
D_MODEL = 1024
SSD_DI = 2048
SSD_HD = 64
SSD_H = 32
SSD_G = 8
SSD_N = 128
SSD_R = SSD_H // SSD_G
SSD_CONV = SSD_DI + 2 * SSD_G * SSD_N
GDN_H = 8
GDN_DK = 128
GDN_DV = 256
GDN_QK = GDN_H * GDN_DK
GDN_VW = GDN_H * GDN_DV
GDN_CONV = 2 * GDN_QK + GDN_VW
SBA_H = 16
SBA_HD = 64
SBA_W = SBA_H * SBA_HD
DEPTH = 4
DN_ALPHA = (2 * DEPTH) ** 0.25

LANES = 128
LOG2E = 1.4426950408889634
CHUNK = 128
VMEM_LIMIT = 56 * 1024 * 1024


def _cparams(sem):
    return pltpu.CompilerParams(dimension_semantics=sem, vmem_limit_bytes=VMEM_LIMIT)


def _dot(a, b):
    return jnp.dot(a, b, preferred_element_type=F32)


def _dot_nt(a, b):
    return lax.dot_general(a, b, (((1,), (1,)), ((), ())), preferred_element_type=F32)


def _split(x, terms):
    parts = []
    r = x
    for i in range(terms):
        p = r.astype(BF16)
        parts.append(p)
        if i + 1 < terms:
            r = r - p.astype(F32)
    return parts


def _dot_x01(x, e, terms=3):
    m = x.shape[0]
    parts = _split(x, terms)
    r = _dot(jnp.concatenate(parts, axis=0), e)
    out = r[0:m]
    for i in range(1, terms):
        out = out + r[i * m:(i + 1) * m]
    return out


def _dot_x01_stacked(x, e_stacked, terms=3):
    return _dot(jnp.concatenate(_split(x, terms), axis=1), e_stacked)


def _dot_01x(e, x, terms=3):
    n = x.shape[1]
    parts = _split(x, terms)
    r = _dot(e, jnp.concatenate(parts, axis=1))
    out = r[:, 0:n]
    for i in range(1, terms):
        out = out + r[:, i * n:(i + 1) * n]
    return out


def _sigmoid(x):
    return 1.0 / (1.0 + jnp.exp(-x))


def _silu(x):
    return x * _sigmoid(x)


def _softplus(x):
    return jnp.maximum(x, 0.0) + jnp.log1p(jnp.exp(-jnp.abs(x)))


def _tri(n, strict):
    r = lax.broadcasted_iota(jnp.int32, (n, n), 0)
    c = lax.broadcasted_iota(jnp.int32, (n, n), 1)
    return (c < r) if strict else (c <= r)


def _mm_kernel(x_ref, w_ref, o_ref, xb_ref):
    @pl.when(pl.program_id(1) == 0)
    def _():
        xb_ref[...] = x_ref[...].astype(BF16)

    o_ref[...] = _dot(xb_ref[...], w_ref[...]).astype(o_ref.dtype)


def _pick(n, cands):
    for c in cands:
        if n % c == 0:
            return c
    return n


def _matmul(x, w, name):
    m, k = x.shape
    n = w.shape[1]
    tm = _pick(m, (1536, 1024, 768, 512, 384, 256, 128))
    tn = _pick(n, (1024, 512, 256, 128))
    return pl.pallas_call(
        _mm_kernel,
        out_shape=jax.ShapeDtypeStruct((m, n), F32),
        grid=(m // tm, n // tn),
        in_specs=[pl.BlockSpec((tm, k), lambda i, j: (i, 0)),
                  pl.BlockSpec((k, tn), lambda i, j: (0, j))],
        out_specs=pl.BlockSpec((tm, tn), lambda i, j: (i, j)),
        scratch_shapes=[pltpu.VMEM((tm, k), BF16)],
        compiler_params=_cparams(("parallel", "arbitrary")),
        name=name,
    )(x, w)


def _out_ln_kernel(y_ref, w_ref, x_ref, g_ref, b_ref, o_ref):
    f = _dot(y_ref[...].astype(BF16), w_ref[...])
    r = DN_ALPHA * x_ref[...] + f
    mu = jnp.mean(r, axis=-1, keepdims=True)
    d = r - mu
    var = jnp.mean(d * d, axis=-1, keepdims=True)
    o_ref[...] = d * lax.rsqrt(var + EPS) * g_ref[...] + b_ref[...]


def _out_ln(y, w, x, g, b, name):
    m, k = y.shape
    d = w.shape[1]
    tm = _pick(m, (512, 384, 256, 128))
    return pl.pallas_call(
        _out_ln_kernel,
        out_shape=jax.ShapeDtypeStruct((m, d), F32),
        grid=(m // tm,),
        in_specs=[pl.BlockSpec((tm, k), lambda i: (i, 0)),
                  pl.BlockSpec((k, d), lambda i: (0, 0)),
                  pl.BlockSpec((tm, d), lambda i: (i, 0)),
                  pl.BlockSpec((1, d), lambda i: (0, 0)),
                  pl.BlockSpec((1, d), lambda i: (0, 0))],
        out_specs=pl.BlockSpec((tm, d), lambda i: (i, 0)),
        compiler_params=_cparams(("parallel",)),
        name=name,
    )(y, w, x, g.reshape(1, d), b.reshape(1, d))


def _conv_chunk(ext_ref, parts, pad, c, t, convw_ref, tail_ref, is_last):
    ncol = ext_ref.shape[1]

    @pl.when(c == 0)
    def _():
        ext_ref[0:8, :] = jnp.zeros((8, ncol), F32)

    @pl.when(c > 0)
    def _():
        ext_ref[0:8, :] = ext_ref[t:t + 8, :]

    off = 0
    for p in parts:
        w = p.shape[1]
        ext_ref[8:8 + t, off:off + w] = p[...]
        off += w

    @pl.when(is_last)
    def _():
        tail_ref[0] = ext_ref[t - pad + 5:t - pad + 8, :]

    full = ext_ref[...]
    acc = convw_ref[CONV_W - 1:CONV_W, :] * full[8:8 + t]
    for s in range(1, CONV_W):
        acc = acc + convw_ref[CONV_W - 1 - s:CONV_W - s, :] * pltpu.roll(full, s, axis=0)[8:8 + t]
    return acc


def _ssd_prompt_kernel(pad, z_ref, xs_ref, bc_ref, x_ref, wdt_ref, convw_ref, convb_ref,
                       dtb_ref, aneg_ref, dskip_ref, normw_ref, expand_ref,
                       yn_ref, hfin_ref, tail_ref, ext_ref, ht_ref, y_ref):
    c = pl.program_id(1)
    is_last = c == pl.num_programs(1) - 1
    t = z_ref.shape[0]
    gw = SSD_R * SSD_HD

    @pl.when(c == 0)
    def _():
        ht_ref[...] = jnp.zeros(ht_ref.shape, F32)

    row = lax.broadcasted_iota(jnp.int32, (t, 1), 0)
    valid = jnp.logical_or(jnp.logical_not(is_last), row < t - pad)

    conv = _conv_chunk(ext_ref, (xs_ref, bc_ref), pad, c, t, convw_ref, tail_ref, is_last)
    xbc = _silu(conv + convb_ref[...])
    xs = xbc[:, :SSD_DI]
    bm = xbc[:, SSD_DI:SSD_DI + SSD_G * SSD_N]
    cm = xbc[:, SSD_DI + SSD_G * SSD_N:]

    dt_raw = _dot(x_ref[...].astype(BF16), wdt_ref[...])
    dt = jnp.where(valid, _softplus(dt_raw + dtb_ref[...]), 0.0)
    a = dt * aneg_ref[...]
    ltri = jnp.where(_tri(t, False), 1.0, 0.0).astype(BF16)
    acum = _dot_01x(ltri, a)
    acum_t = acum.T

    both = _dot_x01_stacked(jnp.concatenate([dt, acum], axis=0), expand_ref[...])
    dt_e = both[0:t]
    acum_e = both[t:2 * t]
    last_e = acum_e[t - 1:t, :]
    eac_e = jnp.exp(acum_e)
    dte_e = jnp.exp(last_e - acum_e)
    cd_e = jnp.exp(last_e)

    xdt = xs * dt_e
    xw = xdt * dte_e
    tri = _tri(t, False)
    lane_g = lax.broadcasted_iota(jnp.int32, (1, gw), 1) // SSD_HD

    for g in range(SSD_G):
        gs = slice(g * gw, (g + 1) * gw)
        ns = slice(g * SSD_N, (g + 1) * SSD_N)
        bg = bm[:, ns]
        cg = cm[:, ns].astype(BF16)
        cb = _dot_nt(cg, bg.astype(BF16))
        xdt_g = xdt[:, gs].astype(BF16)
        ht = ht_ref[g]
        y_g = _dot(cg, ht.astype(BF16)) * eac_e[:, gs]
        for r in range(SSD_R):
            h = g * SSD_R + r
            diff = acum[:, h:h + 1] - acum_t[h:h + 1, :]
            seg = jnp.exp(jnp.where(tri, diff, -jnp.inf))
            yd = _dot((cb * seg).astype(BF16), xdt_g)
            y_g = y_g + jnp.where(lane_g == r, yd, 0.0)
        y_g = y_g + dskip_ref[:, gs] * xs[:, gs]
        y_ref[:, gs] = y_g
        ht_ref[g] = ht * cd_e[:, gs] + _dot(bg.T.astype(BF16), xw[:, gs].astype(BF16))

    y = y_ref[...] * _silu(z_ref[...])
    ms = jnp.mean(y * y, axis=-1, keepdims=True)
    yn_ref[...] = (y * lax.rsqrt(ms + EPS) * normw_ref[...]).astype(yn_ref.dtype)

    @pl.when(is_last)
    def _():
        for g in range(SSD_G):
            hfin_ref[0, g * gw:(g + 1) * gw, :] = ht_ref[g].T


def _ssd_consts(w):
    in_w, conv_w, conv_b, dt_bias, a_log, d_skip, norm_w, out_w = w
    main = in_w[:, :SSD_DI + SSD_CONV].astype(BF16)
    wdt = jnp.pad(in_w[:, SSD_DI + SSD_CONV:], ((0, 0), (0, LANES - SSD_H))).astype(BF16)
    dtb = jnp.pad(dt_bias.astype(F32), (0, LANES - SSD_H)).reshape(1, LANES)
    aneg = jnp.pad(-jnp.exp(a_log.astype(F32)), (0, LANES - SSD_H)).reshape(1, LANES)
    dskip_e = jnp.repeat(d_skip.astype(F32), SSD_HD).reshape(1, SSD_DI)
    hh = lax.broadcasted_iota(jnp.int32, (LANES, SSD_DI), 0)
    cc = lax.broadcasted_iota(jnp.int32, (LANES, SSD_DI), 1) // SSD_HD
    expand = (hh == cc).astype(BF16)
    return dict(main=main, wdt=wdt, dtb=dtb, aneg=aneg, dskip_e=dskip_e, expand=expand,
                expand3=jnp.concatenate([expand] * 3, axis=0),
                conv_w=conv_w.astype(F32), conv_b=conv_b.astype(F32).reshape(1, SSD_CONV),
                norm_w=norm_w.astype(F32).reshape(1, SSD_DI), out_w=out_w.astype(BF16))


def _ssd_prompt(xp, cs, pad, nb, lp, name):
    t = CHUNK
    nc = lp // t
    proj = _matmul(xp, cs["main"], name + "_in")
    const = lambda b, c: (0, 0)
    rows = lambda b, c: (b * nc + c, 0)
    yn, hfin, tail = pl.pallas_call(
        functools.partial(_ssd_prompt_kernel, pad),
        out_shape=(jax.ShapeDtypeStruct((nb * lp, SSD_DI), BF16),
                   jax.ShapeDtypeStruct((nb, SSD_DI, SSD_N), F32),
                   jax.ShapeDtypeStruct((nb, CONV_W - 1, SSD_CONV), F32)),
        grid=(nb, nc),
        in_specs=[pl.BlockSpec((t, SSD_DI), lambda b, c: (b * nc + c, 0)),
                  pl.BlockSpec((t, SSD_DI), lambda b, c: (b * nc + c, 1)),
                  pl.BlockSpec((t, SSD_DI), lambda b, c: (b * nc + c, 2)),
                  pl.BlockSpec((t, D_MODEL), rows),
                  pl.BlockSpec((D_MODEL, LANES), const),
                  pl.BlockSpec((CONV_W, SSD_CONV), const),
                  pl.BlockSpec((1, SSD_CONV), const),
                  pl.BlockSpec((1, LANES), const),
                  pl.BlockSpec((1, LANES), const),
                  pl.BlockSpec((1, SSD_DI), const),
                  pl.BlockSpec((1, SSD_DI), const),
                  pl.BlockSpec((3 * LANES, SSD_DI), const)],
        out_specs=(pl.BlockSpec((t, SSD_DI), rows),
                   pl.BlockSpec((1, SSD_DI, SSD_N), lambda b, c: (b, 0, 0)),
                   pl.BlockSpec((1, CONV_W - 1, SSD_CONV), lambda b, c: (b, 0, 0))),
        scratch_shapes=[pltpu.VMEM((t + 8, SSD_CONV), F32),
                        pltpu.VMEM((SSD_G, SSD_N, SSD_R * SSD_HD), F32),
                        pltpu.VMEM((t, SSD_DI), F32)],
        compiler_params=_cparams(("parallel", "arbitrary")),
        name=name,
    )(proj, proj, proj, xp, cs["wdt"], cs["conv_w"], cs["conv_b"], cs["dtb"], cs["aneg"],
      cs["dskip_e"], cs["norm_w"], cs["expand3"])
    return yn, hfin.reshape(nb, SSD_H, SSD_HD, SSD_N), tail


INV_BLOCK = 16


def _unit_lower_inverse(mats, t):
    n = range(len(mats))
    r = lax.broadcasted_iota(jnp.int32, (t, t), 0)
    c = lax.broadcasted_iota(jnp.int32, (t, t), 1)
    b = INV_BLOCK
    diag = (r // b) == (c // b)
    eye = jnp.where(r == c, 1.0, 0.0)
    ap = [jnp.where(diag, a, 0.0) for a in mats]
    p = [eye - ap[i] for i in n]
    for _ in range(int(math.log2(b)) - 1):
        apb = [ap[i].astype(BF16) for i in n]
        ap = [_dot(apb[i], apb[i]) for i in n]
        prod = [_dot(p[i].astype(BF16), ap[i].astype(BF16)) for i in n]
        p = [p[i] + prod[i] for i in n]
    while b < t:
        sel = jnp.logical_and((r // (2 * b)) == (c // (2 * b)), (r // b) != (c // b))
        off = [jnp.where(sel, a, 0.0).astype(BF16) for a in mats]
        pb = [p[i].astype(BF16) for i in n]
        left = [_dot(pb[i], off[i]).astype(BF16) for i in n]
        corr = [_dot(left[i], pb[i]) for i in n]
        p = [p[i] - corr[i] for i in n]
        b *= 2
    return p


def _gdn_prompt_kernel(pad, q_ref, k_ref, v_ref, z_ref, x_ref, wa_ref, wb_ref, convw_ref,
                       dtb_ref, aneg_ref, normw_ref,
                       on_ref, sfin_ref, tail_ref, ext_ref, s_ref):
    c = pl.program_id(1)
    is_last = c == pl.num_programs(1) - 1
    t = q_ref.shape[0]

    @pl.when(c == 0)
    def _():
        s_ref[...] = jnp.zeros(s_ref.shape, F32)

    row = lax.broadcasted_iota(jnp.int32, (t, 1), 0)
    valid = jnp.logical_or(jnp.logical_not(is_last), row < t - pad)

    conv = _conv_chunk(ext_ref, (q_ref, k_ref, v_ref), pad, c, t, convw_ref, tail_ref, is_last)
    qkv = _silu(conv)

    xb = x_ref[...].astype(BF16)
    a_raw = _dot(xb, wa_ref[...])
    b_raw = _dot(xb, wb_ref[...])
    gate = jnp.where(valid, aneg_ref[...] * _softplus(a_raw + dtb_ref[...]), 0.0)
    beta = jnp.where(valid, _sigmoid(b_raw), 0.0)
    ltri = jnp.where(_tri(t, False), 1.0, 0.0).astype(BF16)
    gc = _dot_01x(ltri, gate)
    gc_t = gc.T
    tri = _tri(t, False)
    strict = _tri(t, True)

    heads = range(GDN_H)
    q, k, v = [], [], []
    for h in heads:
        qh = qkv[:, h * GDN_DK:(h + 1) * GDN_DK]
        kh = qkv[:, GDN_QK + h * GDN_DK:GDN_QK + (h + 1) * GDN_DK]
        q.append(qh * lax.rsqrt(jnp.sum(qh * qh, axis=-1, keepdims=True) + 1e-6) * (GDN_DK ** -0.5))
        k.append(kh * lax.rsqrt(jnp.sum(kh * kh, axis=-1, keepdims=True) + 1e-6))
        v.append(qkv[:, 2 * GDN_QK + h * GDN_DV:2 * GDN_QK + (h + 1) * GDN_DV])
    gcol = [gc[:, h:h + 1] for h in heads]
    bcol = [beta[:, h:h + 1] for h in heads]
    dec = [jnp.exp(jnp.where(tri, gcol[h] - gc_t[h:h + 1, :], -jnp.inf)) for h in heads]
    eg = [jnp.exp(gcol[h]) for h in heads]
    kb = [k[h] * bcol[h] for h in heads]
    kbf = [k[h].astype(BF16) for h in heads]
    kk = [_dot_nt(kb[h].astype(BF16), kbf[h]) for h in heads]
    qk = [_dot_nt(q[h].astype(BF16), kbf[h]) for h in heads]
    a = [jnp.where(strict, kk[h] * dec[h], 0.0) for h in heads]
    qk = [(qk[h] * dec[h]).astype(BF16) for h in heads]
    tinv = [m.astype(BF16) for m in _unit_lower_inverse(a, t)]
    u = [_dot(tinv[h], (v[h] * bcol[h]).astype(BF16)) for h in heads]
    w = [_dot(tinv[h], (kb[h] * eg[h]).astype(BF16)) for h in heads]
    s = [s_ref[h] for h in heads]
    sb = [s[h].astype(BF16) for h in heads]
    ws = [_dot(w[h].astype(BF16), sb[h]) for h in heads]
    qs = [_dot((q[h] * eg[h]).astype(BF16), sb[h]) for h in heads]
    vnb = [(u[h] - ws[h]).astype(BF16) for h in heads]
    o = [qs[h] + _dot(qk[h], vnb[h]) for h in heads]
    gl = [gc[t - 1:t, h:h + 1] for h in heads]
    kd = [(k[h] * jnp.exp(gl[h] - gcol[h])).T.astype(BF16) for h in heads]
    upd = [_dot(kd[h], vnb[h]) for h in heads]
    for h in heads:
        s_ref[h] = s[h] * jnp.exp(gl[h]) + upd[h]
        ms = jnp.mean(o[h] * o[h], axis=-1, keepdims=True)
        zh = z_ref[:, h * GDN_DV:(h + 1) * GDN_DV]
        on_ref[:, h * GDN_DV:(h + 1) * GDN_DV] = (
            o[h] * lax.rsqrt(ms + EPS) * normw_ref[...] * _silu(zh)).astype(on_ref.dtype)

    @pl.when(is_last)
    def _():
        sfin_ref[0] = s_ref[...]


def _gdn_consts(w):
    in_w, conv_w, dt_bias, a_log, norm_w, out_w = w
    nmain = GDN_CONV + GDN_VW
    main = in_w[:, :nmain].astype(BF16)
    wa = jnp.pad(in_w[:, nmain:nmain + GDN_H], ((0, 0), (0, LANES - GDN_H))).astype(BF16)
    wb = jnp.pad(in_w[:, nmain + GDN_H:], ((0, 0), (0, LANES - GDN_H))).astype(BF16)
    dtb = jnp.pad(dt_bias.astype(F32), (0, LANES - GDN_H)).reshape(1, LANES)
    aneg = jnp.pad(-jnp.exp(a_log.astype(F32)), (0, LANES - GDN_H)).reshape(1, LANES)
    return dict(main=main, wa=wa, wb=wb, dtb=dtb, aneg=aneg, conv_w=conv_w.astype(F32),
                norm_w=norm_w.astype(F32).reshape(1, GDN_DV), out_w=out_w.astype(BF16))


def _gdn_prompt(xp, cs, pad, nb, lp, name):
    t = CHUNK
    nc = lp // t
    proj = _matmul(xp, cs["main"], name + "_in")
    const = lambda b, c: (0, 0)
    rows = lambda b, c: (b * nc + c, 0)
    on, sfin, tail = pl.pallas_call(
        functools.partial(_gdn_prompt_kernel, pad),
        out_shape=(jax.ShapeDtypeStruct((nb * lp, GDN_VW), BF16),
                   jax.ShapeDtypeStruct((nb, GDN_H, GDN_DK, GDN_DV), F32),
                   jax.ShapeDtypeStruct((nb, CONV_W - 1, GDN_CONV), F32)),
        grid=(nb, nc),
        in_specs=[pl.BlockSpec((t, GDN_QK), lambda b, c: (b * nc + c, 0)),
                  pl.BlockSpec((t, GDN_QK), lambda b, c: (b * nc + c, 1)),
                  pl.BlockSpec((t, GDN_VW), lambda b, c: (b * nc + c, 1)),
                  pl.BlockSpec((t, GDN_VW), lambda b, c: (b * nc + c, 2)),
                  pl.BlockSpec((t, D_MODEL), rows),
                  pl.BlockSpec((D_MODEL, LANES), const),
                  pl.BlockSpec((D_MODEL, LANES), const),
                  pl.BlockSpec((CONV_W, GDN_CONV), const),
                  pl.BlockSpec((1, LANES), const),
                  pl.BlockSpec((1, LANES), const),
                  pl.BlockSpec((1, GDN_DV), const)],
        out_specs=(pl.BlockSpec((t, GDN_VW), rows),
                   pl.BlockSpec((1, GDN_H, GDN_DK, GDN_DV), lambda b, c: (b, 0, 0, 0)),
                   pl.BlockSpec((1, CONV_W - 1, GDN_CONV), lambda b, c: (b, 0, 0))),
        scratch_shapes=[pltpu.VMEM((t + 8, GDN_CONV), F32),
                        pltpu.VMEM((GDN_H, GDN_DK, GDN_DV), F32)],
        compiler_params=_cparams(("parallel", "arbitrary")),
        name=name,
    )(proj, proj, proj, proj, xp, cs["wa"], cs["wb"], cs["conv_w"], cs["dtb"], cs["aneg"],
      cs["norm_w"])
    return on, sfin, tail


SBA_GROUP = 8


def _sba_prompt_kernel(bias_ref, q_ref, k_ref, v_ref, z_ref, y_ref, acc_ref, car_ref):
    hg = pl.program_id(1)
    qi = pl.program_id(2)
    t = q_ref.shape[0]
    hw = 2 * SBA_HD
    nheads = acc_ref.shape[0]
    first = lax.broadcasted_iota(jnp.int32, (1, hw), 1) < SBA_HD
    qh = []
    for p in range(nheads // 2):
        q = q_ref[:, p * hw:(p + 1) * hw] * (SBA_HD ** -0.5 * LOG2E)
        qh.append(jnp.where(first, q, 0.0).astype(BF16))
        qh.append(jnp.where(first, 0.0, q).astype(BF16))
    bias = [bias_ref[hg * nheads + h] * LOG2E for h in range(nheads)]
    r = lax.broadcasted_iota(jnp.int32, (t, 2 * t), 0)
    c = lax.broadcasted_iota(jnp.int32, (t, 2 * t), 1)
    later_ones = jnp.where(jnp.logical_or(c < r, c >= t), 1.0, 0.0).astype(BF16)
    acc_ref[...] = jnp.zeros(acc_ref.shape, F32)
    car_ref[...] = jnp.zeros(car_ref.shape, F32)

    def block(js, vis):
        heads = range(nheads)
        nj = range(len(js))
        starts = [pl.multiple_of(j * t, t) for j in js]
        kb = [[k_ref[pl.ds(st, t), p * hw:(p + 1) * hw].astype(BF16) for p in range(nheads // 2)]
              for st in starts]
        vb = [[v_ref[pl.ds(st, t), p * hw:(p + 1) * hw].astype(BF16) for p in range(nheads // 2)]
              for st in starts]
        z = [[_dot_nt(qh[h], kb[i][h // 2]) + bias[h] for h in heads] for i in nj]
        sp = [[jnp.maximum(z[i][h], 0.0) + jnp.log2(1.0 + jnp.exp2(-jnp.abs(z[i][h]))) for h in heads]
              for i in nj]
        cat = []
        for i in nj:
            row = []
            for h in heads:
                lk = -sp[i][h] if vis is None else jnp.where(vis, -sp[i][h], 0.0)
                row.append(lk.astype(BF16))
            cat.append(row)
        res = [[_dot(cat[i][h], later_ones) for h in heads] for i in nj]
        car = [car_ref[h] for h in heads]
        a = []
        for i in nj:
            row = []
            for h in heads:
                ah = jnp.exp2(z[i][h] - sp[i][h] + (res[i][h][:, 0:t] + car[h]))
                row.append((ah if vis is None else jnp.where(vis, ah, 0.0)).astype(BF16))
                car[h] = car[h] + res[i][h][:, t:2 * t]
            a.append(row)
        o = [[_dot(a[i][h], vb[i][h // 2]) for h in heads] for i in nj]
        for h in heads:
            tot = o[0][h]
            for i in nj[1:]:
                tot = tot + o[i][h]
            acc_ref[h] += tot
            car_ref[h] = car[h]

    qrow = lax.broadcasted_iota(jnp.int32, (t, t), 0)
    klane = lax.broadcasted_iota(jnp.int32, (t, t), 1)
    block([qi], klane < qrow)

    def body(jj, carry):
        j = qi - 1 - 2 * jj
        block([j, j - 1], None)
        return carry

    lax.fori_loop(0, qi // 2, body, 0)

    @pl.when(qi % 2 == 1)
    def _():
        block([0], None)

    for p in range(nheads // 2):
        o = jnp.where(first, acc_ref[2 * p], acc_ref[2 * p + 1])
        y_ref[:, p * hw:(p + 1) * hw] = (o * _silu(z_ref[:, p * hw:(p + 1) * hw])).astype(y_ref.dtype)


def _sba_prompt(xp, in_w, sb_bias, nb, lp, name):
    t = CHUNK
    nq = lp // t
    gw = SBA_GROUP * SBA_HD
    ngrp = SBA_H // SBA_GROUP
    proj = _matmul(xp, in_w, name + "_in")
    y = pl.pallas_call(
        _sba_prompt_kernel,
        out_shape=jax.ShapeDtypeStruct((nb * lp, SBA_W), BF16),
        grid=(nb, ngrp, nq),
        in_specs=[pl.BlockSpec(memory_space=pltpu.SMEM),
                  pl.BlockSpec((t, gw), lambda b, p, i: (b * nq + i, p)),
                  pl.BlockSpec((lp, gw), lambda b, p, i: (b, ngrp + p)),
                  pl.BlockSpec((lp, gw), lambda b, p, i: (b, 2 * ngrp + p)),
                  pl.BlockSpec((t, gw), lambda b, p, i: (b * nq + i, 3 * ngrp + p))],
        out_specs=pl.BlockSpec((t, gw), lambda b, p, i: (b * nq + i, p)),
        scratch_shapes=[pltpu.VMEM((SBA_GROUP, t, 2 * SBA_HD), F32),
                        pltpu.VMEM((SBA_GROUP, t, t), F32)],
        compiler_params=_cparams(("parallel", "parallel", "arbitrary")),
        name=name,
    )(sb_bias.astype(F32), proj, proj, proj, proj)
    return y, proj


DEC_ROWS = 32
DEC_SEQS = 8


def _dec_conv(parts, st_ref, convw_ref, newst_ref):
    raw = jnp.concatenate([p[...] for p in parts], axis=1)
    conv = convw_ref[CONV_W - 1:CONV_W, :] * raw
    for j in range(CONV_W - 1):
        conv = conv + convw_ref[j:j + 1, :] * st_ref[j]
    for j in range(CONV_W - 2):
        newst_ref[j] = st_ref[j + 1]
    newst_ref[CONV_W - 2] = raw
    return conv


def _ssd_dec_pre_kernel(xs_ref, bc_ref, x_ref, st_ref, wdt_ref, convw_ref, convb_ref, dtb_ref,
                        aneg_ref, expand_ref, newst_ref, xsa_ref, xdt_ref, bca_ref, dec_ref):
    conv = _dec_conv((xs_ref, bc_ref), st_ref, convw_ref, newst_ref)
    xbc = _silu(conv + convb_ref[...])
    dt = _softplus(_dot(x_ref[...].astype(BF16), wdt_ref[...]) + dtb_ref[...])
    dec_ref[...] = jnp.exp(dt * aneg_ref[...])
    xsa = xbc[:, :SSD_DI]
    xsa_ref[...] = xsa
    xdt_ref[...] = xsa * _dot_x01(dt, expand_ref[...])
    bca_ref[...] = xbc[:, SSD_DI:]


def _ssd_dec_state_kernel(dec_ref, h_ref, xdt_ref, bm_ref, cm_ref, xsa_ref, z_ref, dskip_ref,
                          normw_ref, hout_ref, yn_ref):
    blk = pl.program_id(0)
    nseq = h_ref.shape[0]
    gw = SSD_R * SSD_HD
    nrow = nseq * SSD_G
    gmask = (lax.broadcasted_iota(jnp.int32, (SSD_G, SSD_DI), 1) // gw
             == lax.broadcasted_iota(jnp.int32, (SSD_G, SSD_DI), 0))
    pieces = [jnp.where(gmask, jnp.broadcast_to(xdt_ref[i:i + 1, :], (SSD_G, SSD_DI)), 0.0)
              for i in range(nseq)]
    pieces.append(jnp.zeros((LANES - nrow, SSD_DI), F32))
    a_t = jnp.concatenate(pieces, axis=0).T.astype(BF16)
    bm = jnp.concatenate([bm_ref[...], jnp.zeros((LANES - nrow, SSD_N), F32)], axis=0)
    rid = lax.broadcasted_iota(jnp.int32, (LANES, 1), 0) // SSD_G
    y_rows = []
    for i in range(nseq):
        outer = _dot(a_t, jnp.where(rid == i, bm, 0.0).astype(BF16))
        c_i = cm_ref[i * SSD_G:(i + 1) * SSD_G, :].astype(BF16)
        y_parts = []
        for g in range(SSD_G):
            hn = []
            for r in range(SSD_R):
                h = g * SSD_R + r
                hs = slice(h * SSD_HD, (h + 1) * SSD_HD)
                d = dec_ref[(blk * nseq + i) * SSD_H + h]
                v = h_ref[i, hs, :] * d + outer[hs, :]
                hout_ref[i, hs, :] = v
                hn.append(v)
            hg = jnp.concatenate(hn, axis=0).astype(BF16)
            y_parts.append(_dot_nt(c_i, hg)[g:g + 1, :])
        y_rows.append(jnp.concatenate(y_parts, axis=1))
    y = jnp.concatenate(y_rows, axis=0) + dskip_ref[...] * xsa_ref[...]
    y = y * _silu(z_ref[...])
    ms = jnp.mean(y * y, axis=-1, keepdims=True)
    yn_ref[...] = y * lax.rsqrt(ms + EPS) * normw_ref[...]


def _ssd_decode(xs, state, conv_state, cs, name):
    db = xs.shape[0]
    rb = _pick(db, (DEC_ROWS, 8))
    proj = _matmul(xs, cs["main"], name + "_in")
    const = lambda i: (0, 0)
    ntap = CONV_W - 1
    st_spec = pl.BlockSpec((ntap, rb, SSD_CONV), lambda i: (0, i, 0))
    newst, xsa, xdt, bca, dec = pl.pallas_call(
        _ssd_dec_pre_kernel,
        out_shape=(jax.ShapeDtypeStruct((ntap, db, SSD_CONV), F32),
                   jax.ShapeDtypeStruct((db, SSD_DI), F32),
                   jax.ShapeDtypeStruct((db, SSD_DI), F32),
                   jax.ShapeDtypeStruct((db, 2 * SSD_G * SSD_N), F32),
                   jax.ShapeDtypeStruct((db, LANES), F32)),
        grid=(db // rb,),
        in_specs=[pl.BlockSpec((rb, SSD_DI), lambda i: (i, 1)),
                  pl.BlockSpec((rb, SSD_DI), lambda i: (i, 2)),
                  pl.BlockSpec((rb, D_MODEL), lambda i: (i, 0)),
                  st_spec,
                  pl.BlockSpec((D_MODEL, LANES), const),
                  pl.BlockSpec((CONV_W, SSD_CONV), const),
                  pl.BlockSpec((1, SSD_CONV), const),
                  pl.BlockSpec((1, LANES), const),
                  pl.BlockSpec((1, LANES), const),
                  pl.BlockSpec((LANES, SSD_DI), const)],
        out_specs=(st_spec,
                   pl.BlockSpec((rb, SSD_DI), lambda i: (i, 0)),
                   pl.BlockSpec((rb, SSD_DI), lambda i: (i, 0)),
                   pl.BlockSpec((rb, 2 * SSD_G * SSD_N), lambda i: (i, 0)),
                   pl.BlockSpec((rb, LANES), lambda i: (i, 0))),
        compiler_params=_cparams(("parallel",)),
        name=name + "_pre",
    )(proj, proj, xs, jnp.transpose(conv_state, (1, 0, 2)), cs["wdt"], cs["conv_w"], cs["conv_b"],
      cs["dtb"], cs["aneg"], cs["expand"])
    ns = DEC_SEQS
    bm = bca[:, :SSD_G * SSD_N].reshape(db * SSD_G, SSD_N)
    cm = bca[:, SSD_G * SSD_N:].reshape(db * SSD_G, SSD_N)
    hout, yn = pl.pallas_call(
        _ssd_dec_state_kernel,
        out_shape=(jax.ShapeDtypeStruct((db, SSD_DI, SSD_N), F32),
                   jax.ShapeDtypeStruct((db, SSD_DI), F32)),
        grid=(db // ns,),
        in_specs=[pl.BlockSpec(memory_space=pltpu.SMEM),
                  pl.BlockSpec((ns, SSD_DI, SSD_N), lambda i: (i, 0, 0)),
                  pl.BlockSpec((ns, SSD_DI), lambda i: (i, 0)),
                  pl.BlockSpec((ns * SSD_G, SSD_N), lambda i: (i, 0)),
                  pl.BlockSpec((ns * SSD_G, SSD_N), lambda i: (i, 0)),
                  pl.BlockSpec((ns, SSD_DI), lambda i: (i, 0)),
                  pl.BlockSpec((ns, SSD_DI), lambda i: (i, 0)),
                  pl.BlockSpec((1, SSD_DI), const),
                  pl.BlockSpec((1, SSD_DI), const)],
        out_specs=(pl.BlockSpec((ns, SSD_DI, SSD_N), lambda i: (i, 0, 0)),
                   pl.BlockSpec((ns, SSD_DI), lambda i: (i, 0))),
        compiler_params=_cparams(("parallel",)),
        name=name + "_state",
    )(dec[:, :SSD_H].reshape(db * SSD_H), state.reshape(db, SSD_DI, SSD_N), xdt, bm, cm, xsa,
      proj, cs["dskip_e"], cs["norm_w"])
    return (yn, hout.reshape(db, SSD_H, SSD_HD, SSD_N),
            jnp.transpose(newst, (1, 0, 2)))


def _gdn_dec_pre_kernel(q_ref, k_ref, v_ref, x_ref, st_ref, wa_ref, wb_ref, convw_ref, dtb_ref,
                        aneg_ref, newst_ref, qn_ref, kn_ref, va_ref, eg_ref, beta_ref, qk_ref):
    conv = _dec_conv((q_ref, k_ref, v_ref), st_ref, convw_ref, newst_ref)
    qkv = _silu(conv)
    xb = x_ref[...].astype(BF16)
    eg_ref[...] = jnp.exp(aneg_ref[...] * _softplus(_dot(xb, wa_ref[...]) + dtb_ref[...]))
    beta_ref[...] = _sigmoid(_dot(xb, wb_ref[...]))
    lane = lax.broadcasted_iota(jnp.int32, (1, LANES), 1)
    qk = jnp.zeros(qk_ref.shape, F32)
    for h in range(GDN_H):
        hs = slice(h * GDN_DK, (h + 1) * GDN_DK)
        q = qkv[:, hs]
        k = qkv[:, GDN_QK + h * GDN_DK:GDN_QK + (h + 1) * GDN_DK]
        q = q * lax.rsqrt(jnp.sum(q * q, axis=-1, keepdims=True) + 1e-6) * (GDN_DK ** -0.5)
        k = k * lax.rsqrt(jnp.sum(k * k, axis=-1, keepdims=True) + 1e-6)
        qn_ref[:, hs] = q
        kn_ref[:, hs] = k
        qk = jnp.where(lane == h, jnp.sum(q * k, axis=-1, keepdims=True), qk)
    qk_ref[...] = qk
    va_ref[...] = qkv[:, 2 * GDN_QK:]


def _gdn_dec_state_kernel(eg_ref, beta_ref, qk_ref, s_ref, q_ref, k_ref, v_ref, z_ref, normw_ref,
                          sout_ref, on_ref):
    blk = pl.program_id(0)
    nseq = s_ref.shape[0]
    nrow = nseq * GDN_H
    kparts = [k_ref[:, h * GDN_DK:(h + 1) * GDN_DK] for h in range(GDN_H)]
    kparts.append(jnp.zeros((LANES - nrow, GDN_DK), F32))
    k_t = jnp.concatenate(kparts, axis=0).T.astype(BF16)
    rid = lax.broadcasted_iota(jnp.int32, (LANES, 1), 0)
    o_cols = []
    for h in range(GDN_H):
        hs = slice(h * GDN_DK, (h + 1) * GDN_DK)
        vs = slice(h * GDN_DV, (h + 1) * GDN_DV)
        kq = jnp.concatenate([k_ref[:, hs], q_ref[:, hs]], axis=0).astype(BF16)
        o_rows = []
        for i in range(nseq):
            sidx = (blk * nseq + i) * GDN_H + h
            eg = eg_ref[sidx]
            s = s_ref[i, h]
            r = _dot(kq, s.astype(BF16))
            v_new = beta_ref[sidx] * (v_ref[i:i + 1, vs] - eg * r[i:i + 1, :])
            o_rows.append(eg * r[nseq + i:nseq + i + 1, :] + qk_ref[sidx] * v_new)
            rm = jnp.where(rid == h * nseq + i, jnp.broadcast_to(v_new, (LANES, GDN_DV)), 0.0)
            sout_ref[i, h] = s * eg + _dot(k_t, rm.astype(BF16))
        o = jnp.concatenate(o_rows, axis=0)
        ms = jnp.mean(o * o, axis=-1, keepdims=True)
        o_cols.append(o * lax.rsqrt(ms + EPS) * normw_ref[...] * _silu(z_ref[:, vs]))
    on_ref[...] = jnp.concatenate(o_cols, axis=1)


def _gdn_decode(xs, state, conv_state, cs, name):
    db = xs.shape[0]
    rb = _pick(db, (DEC_ROWS, 8))
    proj = _matmul(xs, cs["main"], name + "_in")
    const = lambda i: (0, 0)
    ntap = CONV_W - 1
    st_spec = pl.BlockSpec((ntap, rb, GDN_CONV), lambda i: (0, i, 0))
    row = lambda w: pl.BlockSpec((rb, w), lambda i: (i, 0))
    newst, qn, kn, va, eg, beta, qk = pl.pallas_call(
        _gdn_dec_pre_kernel,
        out_shape=(jax.ShapeDtypeStruct((ntap, db, GDN_CONV), F32),
                   jax.ShapeDtypeStruct((db, GDN_QK), F32),
                   jax.ShapeDtypeStruct((db, GDN_QK), F32),
                   jax.ShapeDtypeStruct((db, GDN_VW), F32),
                   jax.ShapeDtypeStruct((db, LANES), F32),
                   jax.ShapeDtypeStruct((db, LANES), F32),
                   jax.ShapeDtypeStruct((db, LANES), F32)),
        grid=(db // rb,),
        in_specs=[pl.BlockSpec((rb, GDN_QK), lambda i: (i, 0)),
                  pl.BlockSpec((rb, GDN_QK), lambda i: (i, 1)),
                  pl.BlockSpec((rb, GDN_VW), lambda i: (i, 1)),
                  row(D_MODEL), st_spec,
                  pl.BlockSpec((D_MODEL, LANES), const),
                  pl.BlockSpec((D_MODEL, LANES), const),
                  pl.BlockSpec((CONV_W, GDN_CONV), const),
                  pl.BlockSpec((1, LANES), const),
                  pl.BlockSpec((1, LANES), const)],
        out_specs=(st_spec, row(GDN_QK), row(GDN_QK), row(GDN_VW), row(LANES), row(LANES),
                   row(LANES)),
        compiler_params=_cparams(("parallel",)),
        name=name + "_pre",
    )(proj, proj, proj, xs, jnp.transpose(conv_state, (1, 0, 2)), cs["wa"], cs["wb"], cs["conv_w"],
      cs["dtb"], cs["aneg"])
    ns = DEC_SEQS
    smem = pl.BlockSpec(memory_space=pltpu.SMEM)
    flat = lambda a: a[:, :GDN_H].reshape(db * GDN_H)
    sout, on = pl.pallas_call(
        _gdn_dec_state_kernel,
        out_shape=(jax.ShapeDtypeStruct((db, GDN_H, GDN_DK, GDN_DV), F32),
                   jax.ShapeDtypeStruct((db, GDN_VW), F32)),
        grid=(db // ns,),
        in_specs=[smem, smem, smem,
                  pl.BlockSpec((ns, GDN_H, GDN_DK, GDN_DV), lambda i: (i, 0, 0, 0)),
                  pl.BlockSpec((ns, GDN_QK), lambda i: (i, 0)),
                  pl.BlockSpec((ns, GDN_QK), lambda i: (i, 0)),
                  pl.BlockSpec((ns, GDN_VW), lambda i: (i, 0)),
                  pl.BlockSpec((ns, GDN_VW), lambda i: (i, 2)),
                  pl.BlockSpec((1, GDN_DV), const)],
        out_specs=(pl.BlockSpec((ns, GDN_H, GDN_DK, GDN_DV), lambda i: (i, 0, 0, 0)),
                   pl.BlockSpec((ns, GDN_VW), lambda i: (i, 0))),
        compiler_params=_cparams(("parallel",)),
        name=name + "_state",
    )(flat(eg), flat(beta), flat(qk), state, qn, kn, va, proj, cs["norm_w"])
    return on, sout, jnp.transpose(newst, (1, 0, 2))


DEC_PAGES = 8


def _sba_dec_kernel(pt_ref, q_ref, zg_ref, bias_ref, *rest):
    npg = DEC_PAGES
    k_refs = rest[:npg]
    v_refs = rest[npg:2 * npg]
    y_ref, qb_ref, acc_ref, car_ref = rest[2 * npg:]
    s = pl.program_id(1)
    ps = k_refs[0].shape[3]
    rid = lax.broadcasted_iota(jnp.int32, (SBA_H, 1), 0)

    @pl.when(s == 0)
    def _():
        acc_ref[...] = jnp.zeros(acc_ref.shape, F32)
        car_ref[...] = jnp.zeros(car_ref.shape, F32)
        q = q_ref[0] * (SBA_HD ** -0.5 * LOG2E)
        q = jnp.concatenate([q, jnp.zeros((SBA_H, LANES - SBA_HD), F32)], axis=1)
        qt = jnp.concatenate([q, jnp.zeros((LANES - SBA_H, LANES), F32)], axis=0).T
        for h in range(SBA_H):
            qb_ref[h] = jnp.broadcast_to(qt[0:SBA_HD, h:h + 1], (SBA_HD, ps))

    bias = bias_ref[...] * LOG2E
    r = lax.broadcasted_iota(jnp.int32, (2 * ps, 2 * ps), 0) % ps
    c = lax.broadcasted_iota(jnp.int32, (2 * ps, 2 * ps), 1)
    later_ones = jnp.where(jnp.logical_or(r > c, c >= ps), 1.0, 0.0).astype(BF16)

    pages = range(npg)
    z, sp, cat = [], [], []
    for p in pages:
        zp = bias
        for h in range(SBA_H):
            zh = jnp.sum(k_refs[p][0, h] * qb_ref[h], axis=0, keepdims=True)
            zp = zp + jnp.where(rid == h, zh, 0.0)
        spp = jnp.maximum(zp, 0.0) + jnp.log2(1.0 + jnp.exp2(-jnp.abs(zp)))
        hi = (-spp).astype(BF16)
        cat.append(jnp.concatenate([hi, (-spp - hi.astype(F32)).astype(BF16)], axis=1))
        z.append(zp)
        sp.append(spp)
    res = _dot(jnp.concatenate(cat, axis=0), later_ones)
    car = car_ref[...]
    a = []
    for p in pages:
        rp = res[p * SBA_H:(p + 1) * SBA_H]
        a.append(jnp.exp2(z[p] - sp[p] + (rp[:, 0:ps] + car)))
        car = car + rp[:, ps:2 * ps]
    car_ref[...] = car
    for h in range(SBA_H):
        t = acc_ref[h]
        for p in pages:
            t = t + v_refs[p][0, h] * a[p][h:h + 1, :]
        acc_ref[h] = t

    @pl.when(s == pl.num_programs(1) - 1)
    def _():
        ones = jnp.ones((8, ps), BF16)
        o = jnp.zeros((SBA_H, SBA_HD), F32)
        for h in range(SBA_H):
            parts = _split(acc_ref[h], 3)
            oh = _dot_nt(ones, parts[0]) + _dot_nt(ones, parts[1]) + _dot_nt(ones, parts[2])
            o = o + jnp.where(rid == h, oh[0:1, :], 0.0)
        y_ref[0] = o * _silu(zg_ref[0])


def _sba_decode(xs, cache_k, cache_v, page_table, in_w, sb_bias, name):
    db = xs.shape[0]
    ps = cache_k.shape[1]
    npages = page_table.shape[1]
    npg = DEC_PAGES
    assert npages % npg == 0 and ps == LANES
    proj = _matmul(xs, in_w, name + "_in")
    q = proj[:, :SBA_W].reshape(db, SBA_H, SBA_HD)
    zg = proj[:, 3 * SBA_W:].reshape(db, SBA_H, SBA_HD)
    bias = jnp.broadcast_to(sb_bias.astype(F32)[:, None], (SBA_H, LANES))
    kpool = jnp.transpose(cache_k, (0, 2, 3, 1))
    vpool = jnp.transpose(cache_v, (0, 2, 3, 1))

    def page_spec(r):
        return pl.BlockSpec((1, SBA_H, SBA_HD, ps),
                            lambda b, s, pt: (pt[b * npages + npages - 1 - (s * npg + r)], 0, 0, 0))

    one = pl.BlockSpec((1, SBA_H, SBA_HD), lambda b, s, pt: (b, 0, 0))
    y = pl.pallas_call(
        _sba_dec_kernel,
        out_shape=jax.ShapeDtypeStruct((db, SBA_H, SBA_HD), F32),
        grid_spec=pltpu.PrefetchScalarGridSpec(
            num_scalar_prefetch=1,
            grid=(db, npages // npg),
            in_specs=[one, one, pl.BlockSpec((SBA_H, LANES), lambda b, s, pt: (0, 0))]
                     + [page_spec(r) for r in range(npg)] + [page_spec(r) for r in range(npg)],
            out_specs=one,
            scratch_shapes=[pltpu.VMEM((SBA_H, SBA_HD, ps), F32),
                            pltpu.VMEM((SBA_H, SBA_HD, ps), F32),
                            pltpu.VMEM((SBA_H, LANES), F32)]),
        compiler_params=_cparams(("parallel", "arbitrary")),
        name=name,
    )(page_table.reshape(-1).astype(jnp.int32), q, zg, bias, *([kpool] * npg), *([vpool] * npg))
    return y.reshape(db, SBA_W), proj


def _pad_prompt(x_prompt, meta_tokens):
    nb, seq, d = x_prompt.shape
    n_meta = meta_tokens.shape[0]
    real = n_meta + seq
    pad = (-real) % CHUNK
    lp = pad + real
    meta = jnp.broadcast_to(meta_tokens[None].astype(x_prompt.dtype), (nb, n_meta, d))
    assert CHUNK - pad >= CONV_W - 1
    xp = jnp.concatenate([meta, x_prompt, jnp.zeros((nb, pad, d), x_prompt.dtype)], axis=1)
    return xp.reshape(nb * lp, d), pad, lp


def kernel(x_prompt, x_sample, state_l0_ssm, state_l0_conv, state_l1_delta, state_l1_conv, cache_l2_k, cache_l2_v, state_l3_ssm, state_l3_conv, page_table, meta_tokens, l0_in_w, l0_conv_w, l0_conv_b, l0_dt_bias, l0_a_log, l0_d_skip, l0_norm_w, l0_out_w, l0_ln_g, l0_ln_b, l1_in_w, l1_conv_w, l1_dt_bias, l1_a_log, l1_norm_w, l1_out_w, l1_ln_g, l1_ln_b, l2_in_w, l2_sb_bias, l2_out_w, l2_ln_g, l2_ln_b, l3_in_w, l3_conv_w, l3_conv_b, l3_dt_bias, l3_a_log, l3_d_skip, l3_norm_w, l3_out_w, l3_ln_g, l3_ln_b):
    nb, seq, d = x_prompt.shape
    db = x_sample.shape[0]
    assert x_sample.shape[1] == 1 and d == D_MODEL
    xp, pad, lp = _pad_prompt(x_prompt, meta_tokens)
    real = lp - pad
    xs = x_sample.reshape(db, d)

    def ssd_layer(xp, xs, w, ln_g, ln_b, state, conv_state, name):
        cs = _ssd_consts(w)
        yn, p_ssm, p_conv = _ssd_prompt(xp, cs, pad, nb, lp, name + "_prompt")
        xp = _out_ln(yn, cs["out_w"], xp, ln_g, ln_b, name + "_prompt_out")
        yd, s_ssm, s_conv = _ssd_decode(xs, state, conv_state, cs, name + "_dec")
        xs = _out_ln(yd, cs["out_w"], xs, ln_g, ln_b, name + "_dec_out")
        return xp, xs, p_ssm, p_conv, s_ssm, s_conv.reshape(db, CONV_W - 1, SSD_CONV)

    xp, xs, p_l0_ssm, p_l0_conv, s_l0_ssm, s_l0_conv = ssd_layer(
        xp, xs, (l0_in_w, l0_conv_w, l0_conv_b, l0_dt_bias, l0_a_log, l0_d_skip, l0_norm_w, l0_out_w),
        l0_ln_g, l0_ln_b, state_l0_ssm, state_l0_conv, "l0")

    cs1 = _gdn_consts((l1_in_w, l1_conv_w, l1_dt_bias, l1_a_log, l1_norm_w, l1_out_w))
    on, p_l1_delta, p_l1_conv = _gdn_prompt(xp, cs1, pad, nb, lp, "l1_prompt")
    xp = _out_ln(on, cs1["out_w"], xp, l1_ln_g, l1_ln_b, "l1_prompt_out")
    od, s_l1_delta, s_l1_conv = _gdn_decode(xs, state_l1_delta, state_l1_conv, cs1, "l1_dec")
    xs = _out_ln(od, cs1["out_w"], xs, l1_ln_g, l1_ln_b, "l1_dec_out")

    w2_in = l2_in_w.astype(BF16)
    w2_out = l2_out_w.astype(BF16)
    ya, proj_p = _sba_prompt(xp, w2_in, l2_sb_bias, nb, lp, "l2_prompt")
    xp = _out_ln(ya, w2_out, xp, l2_ln_g, l2_ln_b, "l2_prompt_out")
    proj_p = proj_p.reshape(nb, lp, 4 * SBA_W)[:, :real]
    p_l2_k = proj_p[..., SBA_W:2 * SBA_W].reshape(nb, real, SBA_H, SBA_HD)
    p_l2_v = proj_p[..., 2 * SBA_W:3 * SBA_W].reshape(nb, real, SBA_H, SBA_HD)
    yd, proj_s = _sba_decode(xs, cache_l2_k, cache_l2_v, page_table, w2_in, l2_sb_bias, "l2_dec")
    xs = _out_ln(yd, w2_out, xs, l2_ln_g, l2_ln_b, "l2_dec_out")
    s_l2_k = proj_s[:, SBA_W:2 * SBA_W].reshape(db, 1, SBA_H, SBA_HD)
    s_l2_v = proj_s[:, 2 * SBA_W:3 * SBA_W].reshape(db, 1, SBA_H, SBA_HD)

    xp, xs, p_l3_ssm, p_l3_conv, s_l3_ssm, s_l3_conv = ssd_layer(
        xp, xs, (l3_in_w, l3_conv_w, l3_conv_b, l3_dt_bias, l3_a_log, l3_d_skip, l3_norm_w, l3_out_w),
        l3_ln_g, l3_ln_b, state_l3_ssm, state_l3_conv, "l3")

    y_prompt = xp.reshape(nb, lp, d)[:, meta_tokens.shape[0]:real]
    y_sample = xs.reshape(db, 1, d)
    return (y_prompt, y_sample,
            p_l0_ssm, p_l0_conv, p_l1_delta, p_l1_conv, p_l2_k, p_l2_v, p_l3_ssm, p_l3_conv,
            s_l0_ssm, s_l0_conv, s_l1_delta, s_l1_conv, s_l2_k, s_l2_v, s_l3_ssm, s_l3_conv)
```
